```python
import math
import jax, jax.numpy as jnp
from jax import lax
import numpy as np

D_MODEL = 1024
BATCH = 2
SEQ = 8192
DEPTH = 4
DEC_BATCH = 128
DEC_SEQ = 1
PAST_LEN = 2048
PAGE_SIZE = 128

ATT_HEADS = 8
HEAD_DIM = 64
ATT_WIDTH = ATT_HEADS * HEAD_DIM
POOL_WIDTH = D_MODEL // 4
POOL_GROUPS = 4
POOL_GROUP_DIM = POOL_WIDTH // POOL_GROUPS
POOL_WINDOWS = (2, 4, 8, 16)
POOL_HIST = max(POOL_WINDOWS) - 1
CONV_WIDTH = D_MODEL - ATT_WIDTH - POOL_WIDTH
CONV_KERNEL = 31
CONV_HIST = CONV_KERNEL - 1
IN_WIDTH = 3 * ATT_WIDTH + POOL_WIDTH + 2 * CONV_WIDTH
IN_SPLITS = [ATT_WIDTH, 2 * ATT_WIDTH, 3 * ATT_WIDTH, 3 * ATT_WIDTH + POOL_WIDTH]
MOBA_BLOCK = 256
MOBA_TOPK = 3
Q_CHUNK = 64
N_EXPERTS = 16
N_EXPERT_GROUPS = 4
EXPERTS_PER_GROUP = N_EXPERTS // N_EXPERT_GROUPS
EXPERT_TOPK = 2
D_EXPERT = 512
DEEPNORM_ALPHA = (2 * DEPTH) ** 0.25
DEEPNORM_BETA = (8 * DEPTH) ** -0.25
LN_EPS = 1e-5

kernel_name = "hybrid_moba_pool_conv_moe_step"


def layer_norm(x, g, b):
    xf = x.astype(jnp.float32)
    mu = jnp.mean(xf, axis=-1, keepdims=True)
    var = jnp.mean(jnp.square(xf - mu), axis=-1, keepdims=True)
    y = (xf - mu) * lax.rsqrt(var + LN_EPS) * g.astype(jnp.float32) + b.astype(jnp.float32)
    return y.astype(x.dtype)


def alibi_slopes(n_heads):
    return 2.0 ** (-8.0 * jnp.arange(1, n_heads + 1, dtype=jnp.float32) / n_heads)


def moba_attention(q, k_all, v_all, pos0):
    b, t, h, dh = q.shape
    seq_k = k_all.shape[1]
    nb = -(-seq_k // MOBA_BLOCK)
    pad = nb * MOBA_BLOCK - seq_k
    if pad:
        k_all = jnp.pad(k_all, ((0, 0), (0, pad), (0, 0), (0, 0)))
        v_all = jnp.pad(v_all, ((0, 0), (0, pad), (0, 0), (0, 0)))
    kbh = k_all.reshape(b, nb, MOBA_BLOCK, h, dh).transpose(0, 3, 1, 2, 4)
    vbh = v_all.reshape(b, nb, MOBA_BLOCK, h, dh).transpose(0, 3, 1, 2, 4)
    kmean = jnp.mean(kbh.astype(jnp.float32), axis=3)
    qc = math.gcd(t, Q_CHUNK)
    n_chunks = t // qc
    n_sel = min(MOBA_TOPK, nb)
    slopes = alibi_slopes(h)
    scale = dh ** -0.5
    bi = jnp.arange(b)[:, None, None, None]
    hi = jnp.arange(h)[None, :, None, None]
    offs = jnp.arange(MOBA_BLOCK)

    def chunk(args):
        qch, c = args
        start = pos0 + c * qc
        pos = start + jnp.arange(qc)
        own = start // MOBA_BLOCK
        qh = qch.transpose(0, 2, 1, 3)
        gate = jnp.einsum('bhqd,bhnd->bhqn', qh.astype(jnp.float32), kmean)
        gate = jnp.where(jnp.arange(nb) < own, gate, -jnp.inf)
        _, top_idx = lax.top_k(gate, n_sel)
        top_ok = top_idx < own
        kg = kbh[bi, hi, top_idx]
        vg = vbh[bi, hi, top_idx]
        k_own = lax.dynamic_index_in_dim(kbh, own, axis=2, keepdims=False)
        v_own = lax.dynamic_index_in_dim(vbh, own, axis=2, keepdims=False)
        s_top = jnp.einsum('bhqd,bhqnkd->bhqnk', qh, kg).astype(jnp.float32) * scale
        s_own = jnp.einsum('bhqd,bhkd->bhqk', qh, k_own).astype(jnp.float32) * scale
        d_top = (pos[:, None, None] - (top_idx[..., None] * MOBA_BLOCK + offs)).astype(jnp.float32)
        d_own = pos[:, None] - (own * MOBA_BLOCK + offs)[None, :]
        s_top = jnp.where(top_ok[..., None], s_top - slopes[:, None, None, None] * d_top, -jnp.inf)
        s_own = jnp.where(d_own >= 0, s_own - slopes[:, None, None] * d_own.astype(jnp.float32), -jnp.inf)
        logits = jnp.concatenate([s_top.reshape(b, h, qc, n_sel * MOBA_BLOCK), s_own], axis=-1)
        p = jax.nn.softmax(logits, axis=-1).astype(v_all.dtype)
        p_top = p[..., :n_sel * MOBA_BLOCK].reshape(b, h, qc, n_sel, MOBA_BLOCK)
        p_own = p[..., n_sel * MOBA_BLOCK:]
        return (jnp.einsum('bhqnk,bhqnkd->bqhd', p_top, vg)
                + jnp.einsum('bhqk,bhkd->bqhd', p_own, v_own))

    q_chunks = q.reshape(b, n_chunks, qc, h, dh).transpose(1, 0, 2, 3, 4)
    out = lax.map(chunk, (q_chunks, jnp.arange(n_chunks)))
    return out.transpose(1, 0, 2, 3, 4).reshape(b, t, h * dh)


def pool_mixer(u, prefix, pos0, pool_w, pool_scale):
    b, t, c = u.shape
    ext = jnp.concatenate([prefix.astype(u.dtype), u], axis=1)
    csum = jnp.pad(jnp.cumsum(ext.astype(jnp.float32), axis=1), ((0, 0), (1, 0), (0, 0)))
    n_avail = pos0 + jnp.arange(t) + 1
    pooled = []
    for g, w in enumerate(POOL_WINDOWS):
        sl = slice(g * POOL_GROUP_DIM, (g + 1) * POOL_GROUP_DIM)
        win_sum = (csum[:, POOL_HIST + 1:POOL_HIST + 1 + t, sl]
                   - csum[:, POOL_HIST + 1 - w:POOL_HIST + 1 - w + t, sl])
        cnt = jnp.minimum(n_avail, w).astype(jnp.float32)
        pooled.append(win_sum / cnt[None, :, None])
    pooled = jnp.stack(pooled, axis=2)
    diff = pooled - u.astype(jnp.float32).reshape(b, t, POOL_GROUPS, POOL_GROUP_DIM)
    y = jnp.einsum('btgc,gcd->btgd', diff.astype(u.dtype), pool_w).reshape(b, t, c) * pool_scale
    return y, ext[:, -POOL_HIST:]


def conv_module(u, prefix, conv_w, conv_b, ln_g, ln_b, conv_pw):
    a, g = jnp.split(u, 2, axis=-1)
    glu = a * jax.nn.sigmoid(g)
    ext = jnp.concatenate([prefix.astype(glu.dtype), glu], axis=1)
    y = lax.conv_general_dilated(ext, conv_w[:, None, :].astype(glu.dtype), window_strides=(1,),
                                 padding='VALID', dimension_numbers=('NWC', 'WIO', 'NWC'),
                                 feature_group_count=CONV_WIDTH) + conv_b
    y = jax.nn.silu(layer_norm(y, ln_g, ln_b))
    return y @ conv_pw, ext[:, -CONV_HIST:]


def shared_router(h, router_w, router_bias):
    probs = jax.nn.softmax(jnp.einsum('btd,de->bte', h, router_w).astype(jnp.float32), axis=-1)
    sel = probs + router_bias.astype(jnp.float32)
    grp = sel.reshape(sel.shape[:-1] + (N_EXPERT_GROUPS, EXPERTS_PER_GROUP))
    grp_score = jnp.sum(lax.top_k(grp, EXPERT_TOPK)[0], axis=-1)
    best = jnp.argmax(grp_score, axis=-1)
    in_grp = (jnp.arange(N_EXPERTS) // EXPERTS_PER_GROUP) == best[..., None]
    _, top_e = lax.top_k(jnp.where(in_grp, sel, -jnp.inf), EXPERT_TOPK)
    wts = jnp.take_along_axis(probs, top_e, axis=-1)
    wts = wts / jnp.sum(wts, axis=-1, keepdims=True)
    return jnp.sum(jax.nn.one_hot(top_e, N_EXPERTS, dtype=jnp.float32) * wts[..., None], axis=-2)


def moe_ffn(h, gates, exp_gate, exp_up, exp_down):
    out = jnp.zeros(h.shape, jnp.float32)
    for e in range(N_EXPERTS):
        y = (jax.nn.silu(h @ exp_gate[e]) * (h @ exp_up[e])) @ exp_down[e]
        out = out + gates[..., e:e + 1] * y.astype(jnp.float32)
    return out.astype(h.dtype)


def hybrid_layer(x, pos0, k_past, v_past, pool_prefix, conv_prefix,
                 w_in, pool_w, pool_scale, conv_w, conv_b, conv_ln_g, conv_ln_b, conv_pw,
                 w_out, ln1_g, ln1_b, router_w, router_bias, exp_gate, exp_up, exp_down,
                 ln2_g, ln2_b):
    b, t, _ = x.shape
    proj = x @ w_in
    q, k, v, u_pool, u_conv = jnp.split(proj, IN_SPLITS, axis=-1)
    q = q.reshape(b, t, ATT_HEADS, HEAD_DIM)
    k = k.reshape(b, t, ATT_HEADS, HEAD_DIM)
    v = v.reshape(b, t, ATT_HEADS, HEAD_DIM)
    if k_past is None:
        k_all, v_all = k, v
    else:
        k_all = jnp.concatenate([k_past.astype(k.dtype), k], axis=1)
        v_all = jnp.concatenate([v_past.astype(v.dtype), v], axis=1)
    y_att = moba_attention(q, k_all, v_all, pos0)
    y_pool, pool_state = pool_mixer(u_pool, pool_prefix, pos0, pool_w, pool_scale)
    y_conv, conv_state = conv_module(u_conv, conv_prefix, conv_w, conv_b, conv_ln_g, conv_ln_b, conv_pw)
    mix = jnp.concatenate([y_att, y_pool, y_conv], axis=-1) @ w_out
    x = layer_norm(DEEPNORM_ALPHA * x + mix, ln1_g, ln1_b)
    gates = shared_router(x, router_w, router_bias)
    x = layer_norm(DEEPNORM_ALPHA * x + moe_ffn(x, gates, exp_gate, exp_up, exp_down), ln2_g, ln2_b)
    return x, k, v, pool_state, conv_state


def setup_inputs(seed: int = 0) -> dict:
    key = jax.random.key(seed)
    ks = jax.random.split(key, 26)
    n_pages = PAST_LEN // PAGE_SIZE
    n_used = DEC_BATCH * n_pages
    n_phys = (n_used * 5 + 3) // 4
    f32 = jnp.float32
    nrm = lambda k, shape, s=1.0: jax.random.normal(k, shape, f32) * s
    page_table = jax.random.permutation(ks[6], n_phys)[:n_used].reshape(DEC_BATCH, n_pages).astype(jnp.int32)
    return {
        'x_prompt': nrm(ks[0], (BATCH, SEQ, D_MODEL)),
        'x_sample': nrm(ks[1], (DEC_BATCH, DEC_SEQ, D_MODEL)),
        'cache_k': nrm(ks[2], (n_phys, DEPTH, PAGE_SIZE, ATT_HEADS, HEAD_DIM)),
        'cache_v': nrm(ks[3], (n_phys, DEPTH, PAGE_SIZE, ATT_HEADS, HEAD_DIM)),
        'state_pool': nrm(ks[4], (DEC_BATCH, DEPTH, POOL_HIST, POOL_WIDTH)),
        'state_conv': nrm(ks[5], (DEC_BATCH, DEPTH, CONV_HIST, CONV_WIDTH), 0.5),
        'page_table': page_table,
        'w_in': nrm(ks[7], (DEPTH, D_MODEL, IN_WIDTH), D_MODEL ** -0.5),
        'pool_w': nrm(ks[8], (DEPTH, POOL_GROUPS, POOL_GROUP_DIM, POOL_GROUP_DIM), POOL_GROUP_DIM ** -0.5),
        'pool_scale': 1.0 + nrm(ks[9], (DEPTH, POOL_WIDTH), 0.1),
        'conv_w': nrm(ks[10], (DEPTH, CONV_KERNEL, CONV_WIDTH), CONV_KERNEL ** -0.5),
        'conv_b': nrm(ks[11], (DEPTH, CONV_WIDTH), 0.01),
        'conv_ln_g': 1.0 + nrm(ks[12], (DEPTH, CONV_WIDTH), 0.1),
        'conv_ln_b': nrm(ks[13], (DEPTH, CONV_WIDTH), 0.01),
        'conv_pw': nrm(ks[14], (DEPTH, CONV_WIDTH, CONV_WIDTH), CONV_WIDTH ** -0.5),
        'w_out': nrm(ks[15], (DEPTH, D_MODEL, D_MODEL), D_MODEL ** -0.5 * DEEPNORM_BETA),
        'ln1_g': 1.0 + nrm(ks[16], (DEPTH, D_MODEL), 0.1),
        'ln1_b': nrm(ks[17], (DEPTH, D_MODEL), 0.01),
        'router_w': nrm(ks[18], (D_MODEL, N_EXPERTS), D_MODEL ** -0.5),
        'router_bias': nrm(ks[19], (N_EXPERTS,), 0.01),
        'exp_gate': nrm(ks[20], (DEPTH, N_EXPERTS, D_MODEL, D_EXPERT), D_MODEL ** -0.5),
        'exp_up': nrm(ks[21], (DEPTH, N_EXPERTS, D_MODEL, D_EXPERT), D_MODEL ** -0.5),
        'exp_down': nrm(ks[22], (DEPTH, N_EXPERTS, D_EXPERT, D_MODEL), D_EXPERT ** -0.5 * DEEPNORM_BETA),
        'ln2_g': 1.0 + nrm(ks[23], (DEPTH, D_MODEL), 0.1),
        'ln2_b': nrm(ks[24], (DEPTH, D_MODEL), 0.01),
    }


def reference(x_prompt, x_sample, cache_k, cache_v, state_pool, state_conv, page_table,
              w_in, pool_w, pool_scale, conv_w, conv_b, conv_ln_g, conv_ln_b, conv_pw,
              w_out, ln1_g, ln1_b, router_w, router_bias, exp_gate, exp_up, exp_down,
              ln2_g, ln2_b):
    n_dec, n_pages = page_table.shape
    past_len = n_pages * cache_k.shape[2]
    n_prompt = x_prompt.shape[0]
    zero_pool = jnp.zeros((n_prompt, POOL_HIST, POOL_WIDTH), x_prompt.dtype)
    zero_conv = jnp.zeros((n_prompt, CONV_HIST, CONV_WIDTH), x_prompt.dtype)
    hp, hs = x_prompt, x_sample
    kp_l, vp_l, ks_l, vs_l, pp_l, ps_l, cp_l, cs_l = [], [], [], [], [], [], [], []
    for l in range(DEPTH):
        layer_w = (w_in[l], pool_w[l], pool_scale[l], conv_w[l], conv_b[l], conv_ln_g[l],
                   conv_ln_b[l], conv_pw[l], w_out[l], ln1_g[l], ln1_b[l], router_w, router_bias,
                   exp_gate[l], exp_up[l], exp_down[l], ln2_g[l], ln2_b[l])
        hp, kp, vp, pp, cp = hybrid_layer(hp, 0, None, None, zero_pool, zero_conv, *layer_w)
        k_past = cache_k[page_table, l].reshape(n_dec, past_len, ATT_HEADS, HEAD_DIM)
        v_past = cache_v[page_table, l].reshape(n_dec, past_len, ATT_HEADS, HEAD_DIM)
        hs, ks_, vs_, ps_, cs_ = hybrid_layer(hs, past_len, k_past, v_past, state_pool[:, l],
                                              state_conv[:, l], *layer_w)
        kp_l.append(kp); vp_l.append(vp); pp_l.append(pp); cp_l.append(cp)
        ks_l.append(ks_); vs_l.append(vs_); ps_l.append(ps_); cs_l.append(cs_)
    k_prompt = jnp.stack(kp_l, axis=1)
    v_prompt = jnp.stack(vp_l, axis=1)
    k_sample = jnp.stack(ks_l, axis=1)
    v_sample = jnp.stack(vs_l, axis=1)
    pool_prompt = jnp.stack(pp_l, axis=1)
    pool_sample = jnp.stack(ps_l, axis=1)
    conv_prompt = jnp.stack(cp_l, axis=1)
    conv_sample = jnp.stack(cs_l, axis=1)
    return (hp, hs, k_prompt, v_prompt, k_sample, v_sample, pool_prompt, pool_sample, conv_prompt, conv_sample)
```

```python
import functools
import math

import jax
import jax.numpy as jnp
from jax import lax
from jax.experimental import pallas as pl
from jax.experimental.pallas import tpu as pltpu

F32 = jnp.float32
BF16 = jnp.bfloat16

ATT_HEADS = 8
HEAD_DIM = 64
ATT_WIDTH = ATT_HEADS * HEAD_DIM
POOL_WINDOWS = (2, 4, 8, 16)
POOL_GROUPS = len(POOL_WINDOWS)
POOL_HIST = max(POOL_WINDOWS) - 1
CONV_KERNEL = 31
CONV_HIST = CONV_KERNEL - 1
MOBA_BLOCK = 256
MOBA_TOPK = 3
N_EXPERTS = 16
N_EXPERT_GROUPS = 4
EXPERTS_PER_GROUP = N_EXPERTS // N_EXPERT_GROUPS
EXPERT_TOPK = 2
LN_EPS = 1e-5
NEG_BIG = -1e30

LANES = 128
HEADS_PER_TILE = LANES // HEAD_DIM
POOL_HALO = 16
CONV_HALO = 32
VMEM_LIMIT = 56 * 1024 * 1024

TM_IN = 512
TT_MIX = 512
TM_MOE = 1024
CONV_ROWS = 64


def _params(*sem):
    return pltpu.CompilerParams(dimension_semantics=sem, vmem_limit_bytes=VMEM_LIMIT)


def _sigmoid(x):
    return 1.0 / (1.0 + jnp.exp(-x))


def _silu(x):
    return x * _sigmoid(x)


def _layer_norm(x, g, b):
    mu = jnp.mean(x, axis=-1, keepdims=True)
    xc = x - mu
    var = jnp.mean(xc * xc, axis=-1, keepdims=True)
    return xc * lax.rsqrt(var + LN_EPS) * g + b


def _alibi_slopes(n_heads):
    return 2.0 ** (-8.0 * jnp.arange(1, n_heads + 1, dtype=F32) / n_heads)


def _in_proj_kernel(x_ref, w_ref, q_ref, k_ref, v_ref, up_ref, glu_ref, *aux_refs, tm):
    aw = ATT_WIDTH
    pw = up_ref.shape[-1]
    cw = glu_ref.shape[-1]
    xb = x_ref[...].astype(BF16)

    def seg(lo, width):
        return jnp.dot(xb, w_ref[:, lo:lo + width], preferred_element_type=F32)

    q_ref[...] = seg(0, aw)
    k = seg(aw, aw)
    k_ref[...] = k
    v = seg(2 * aw, aw)
    v_ref[...] = v
    up_ref[...] = seg(3 * aw, pw)
    a = seg(3 * aw + pw, cw)
    g = seg(3 * aw + pw + cw, cw)
    glu_ref[...] = a * _sigmoid(g)
    if aux_refs:
        kb_ref, vb_ref, km_ref = aux_refs
        kb_ref[...] = k.astype(BF16)
        vb_ref[...] = v.astype(BF16)
        for i in range(tm // MOBA_BLOCK):
            km_ref[i] = jnp.mean(k[i * MOBA_BLOCK:(i + 1) * MOBA_BLOCK], axis=0, keepdims=True)


def _in_proj(x, w_bf, *, tm, with_aux):
    m, d = x.shape
    n = w_bf.shape[1]
    aw = ATT_WIDTH
    pw = (n - 3 * aw) // 3
    cw = pw
    row = lambda width: pl.BlockSpec((tm, width), lambda i: (i, 0))
    out_shape = [jax.ShapeDtypeStruct((m, aw), F32)] * 3 + [
        jax.ShapeDtypeStruct((m, pw), F32), jax.ShapeDtypeStruct((m, cw), F32)]
    out_specs = [row(aw)] * 3 + [row(pw), row(cw)]
    if with_aux:
        nblk = tm // MOBA_BLOCK
        out_shape += [jax.ShapeDtypeStruct((m, aw), BF16)] * 2 + [
            jax.ShapeDtypeStruct((m // MOBA_BLOCK, 1, aw), F32)]
        out_specs += [row(aw)] * 2 + [pl.BlockSpec((nblk, 1, aw), lambda i: (i, 0, 0))]
    return pl.pallas_call(
        functools.partial(_in_proj_kernel, tm=tm),
        out_shape=out_shape,
        grid=(m // tm,),
        in_specs=[row(d), pl.BlockSpec((d, n), lambda i: (0, 0))],
        out_specs=out_specs,
        compiler_params=_params("parallel"),
        name="in_proj",
    )(x, w_bf)


def _moba_prompt_kernel(slopes_ref, q_ref, k_ref, v_ref, km_ref, aboff_ref, abdiag_ref,
                        o_ref, qcat_ref, *, n_sel):
    blk = MOBA_BLOCK
    hp = pl.program_id(1)
    qi = pl.program_id(2)
    lane = lax.broadcasted_iota(jnp.int32, (blk, LANES), 1)
    lane_f = lane.astype(F32)
    head_lanes =[(lane >= hh * HEAD_DIM) & (lane < (hh + 1) * HEAD_DIM) for hh in range(HEADS_PER_TILE)]
    q = q_ref[...]
    km = km_ref[0]
    scale = HEAD_DIM ** -0.5

    for hh in range(HEADS_PER_TILE):
        qm = jnp.where(head_lanes[hh], q, 0.0)
        gate = lax.dot_general(qm, km, (((1,), (1,)), ((), ())),
                               precision=lax.Precision.HIGHEST, preferred_element_type=F32)
        gate = jnp.where(lane < qi, gate, -jnp.inf)
        sel = jnp.zeros((blk, LANES), jnp.bool_)
        for _ in range(n_sel):
            mx = jnp.max(gate, axis=1, keepdims=True)
            first = jnp.min(jnp.where(gate == mx, lane_f, float(LANES)), axis=1, keepdims=True)
            pick = (lane_f == first) & (mx > -jnp.inf)
            sel = sel | pick
            gate = jnp.where(pick, -jnp.inf, gate)
        bias = jnp.where(sel, 0.0, NEG_BIG)
        qcat_ref[hh] = jnp.concatenate([(qm * scale).astype(BF16), bias.astype(BF16)], axis=1)

    slopes = [slopes_ref[hp * HEADS_PER_TILE + hh] for hh in range(HEADS_PER_TILE)]

    def attend(kj, vj, marker, ab_ref, dist, carry):
        ms, ls, acc = carry
        kcat = jnp.concatenate([kj, marker], axis=1)
        new_ms, new_ls, alphas, pvs = [], [], [], []
        for hh in range(HEADS_PER_TILE):
            s = lax.dot_general(qcat_ref[hh], kcat, (((1,), (1,)), ((), ())),
                                preferred_element_type=F32) + ab_ref[hh]
            c = -(slopes[hh] * dist)
            m_new = jnp.maximum(ms[hh], jnp.max(s, axis=1, keepdims=True) + c)
            p = jnp.exp(s - (m_new - c))
            alpha = jnp.exp(ms[hh] - m_new)
            new_ls.append(alpha * ls[hh] + jnp.sum(p, axis=1, keepdims=True))
            new_ms.append(m_new)
            alphas.append(alpha)
            pvs.append(jnp.dot(p.astype(BF16), vj, preferred_element_type=F32))
        a_full, pv_full = alphas[-1], pvs[-1]
        for hh in range(HEADS_PER_TILE - 1):
            a_full = jnp.where(head_lanes[hh], alphas[hh], a_full)
            pv_full = jnp.where(head_lanes[hh], pvs[hh], pv_full)
        return tuple(new_ms), tuple(new_ls), acc * a_full + pv_full

    def past_block(j, carry):
        start = pl.multiple_of(j * blk, blk)
        kj = k_ref[0, pl.ds(start, blk), :]
        vj = v_ref[0, pl.ds(start, blk), :]
        marker = jnp.where(lane == j, 1.0, 0.0).astype(BF16)
        dist = ((qi - j) * blk).astype(F32)
        return attend(kj, vj, marker, aboff_ref, dist, carry)

    init = (tuple(jnp.full((blk, 1), -jnp.inf, F32) for _ in range(HEADS_PER_TILE)),
            tuple(jnp.zeros((blk, 1), F32) for _ in range(HEADS_PER_TILE)),
            jnp.zeros((blk, LANES), F32))
    carry = lax.fori_loop(0, qi, past_block, init)
    start = pl.multiple_of(qi * blk, blk)
    ms, ls, acc = attend(k_ref[0, pl.ds(start, blk), :], v_ref[0, pl.ds(start, blk), :],
                         jnp.zeros((blk, LANES), BF16), abdiag_ref, jnp.float32(0.0), carry)
    inv = 1.0 / ls[-1]
    for hh in range(HEADS_PER_TILE - 1):
        inv = jnp.where(head_lanes[hh], 1.0 / ls[hh], inv)
    o_ref[...] = (acc * inv).astype(o_ref.dtype)


def _moba_prompt(q, kb, vb, kmean, b, t):
    blk = MOBA_BLOCK
    nb = t // blk
    assert t % blk == 0 and nb <= LANES
    n_sel = min(MOBA_TOPK, nb)
    slopes = _alibi_slopes(ATT_HEADS)
    r = jnp.arange(blk, dtype=F32)
    rel = r[:, None] - r[None, :]
    ab_off = -slopes[:, None, None] * rel[None]
    ab_diag = jnp.where(rel[None] >= 0, ab_off, NEG_BIG)
    km = jnp.pad(kmean.reshape(b, nb, ATT_WIDTH), ((0, 0), (0, LANES - nb), (0, 0)))
    kb3 = kb.reshape(b, t, ATT_WIDTH)
    vb3 = vb.reshape(b, t, ATT_WIDTH)
    n_hp = ATT_HEADS // HEADS_PER_TILE
    return pl.pallas_call(
        functools.partial(_moba_prompt_kernel, n_sel=n_sel),
        out_shape=jax.ShapeDtypeStruct((b * t, ATT_WIDTH), BF16),
        grid=(b, n_hp, nb),
        in_specs=[
            pl.BlockSpec(memory_space=pltpu.SMEM),
            pl.BlockSpec((blk, LANES), lambda bi, hp, qi: (bi * nb + qi, hp)),
            pl.BlockSpec((1, t, LANES), lambda bi, hp, qi: (bi, 0, hp)),
            pl.BlockSpec((1, t, LANES), lambda bi, hp, qi: (bi, 0, hp)),
            pl.BlockSpec((1, LANES, LANES), lambda bi, hp, qi: (bi, 0, hp)),
            pl.BlockSpec((HEADS_PER_TILE, blk, blk), lambda bi, hp, qi: (hp, 0, 0)),
            pl.BlockSpec((HEADS_PER_TILE, blk, blk), lambda bi, hp, qi: (hp, 0, 0)),
        ],
        out_specs=pl.BlockSpec((blk, LANES), lambda bi, hp, qi: (bi * nb + qi, hp)),
        scratch_shapes=[pltpu.VMEM((HEADS_PER_TILE, blk, 2 * LANES), BF16)],
        compiler_params=_params("parallel", "parallel", "arbitrary"),
        name="moba_prompt",
    )(slopes, q, kb3, vb3, km, ab_off, ab_diag)


def _moba_decode_kernel(pt_ref, q_ref, kn_ref, vn_ref, slopes_ref, toff_ref, *rest,
                        pages_per_block, page, past_len, n_sel):
    k_refs = rest[:pages_per_block]
    v_refs = rest[pages_per_block:2 * pages_per_block]
    o_ref = rest[2 * pages_per_block]
    m_s, l_s, acc_s, gate_s = rest[2 * pages_per_block + 1:]
    n = pl.program_id(1)
    n_blocks = pl.num_programs(1)
    q = q_ref[0]
    qs = q * (HEAD_DIM ** -0.5)
    slopes = slopes_ref[...]
    toff = toff_ref[...]

    scores, ksum = [], jnp.zeros(q.shape, F32)
    for pg in range(pages_per_block):
        k = k_refs[pg][...]
        s = jnp.sum(k * qs[None], axis=-1, keepdims=True)
        base = (past_len - n * MOBA_BLOCK - pg * page).astype(F32)
        scores.append(s - slopes * (base - toff))
        ksum = ksum + jnp.sum(k, axis=0)
    m_blk = scores[0].max(axis=0)
    for s in scores[1:]:
        m_blk = jnp.maximum(m_blk, s.max(axis=0))
    l_blk = jnp.zeros(m_blk.shape, F32)
    acc = jnp.zeros(q.shape, F32)
    for pg in range(pages_per_block):
        p = jnp.exp(scores[pg] - m_blk[None])
        l_blk = l_blk + jnp.sum(p, axis=0)
        acc = acc + jnp.sum(p * v_refs[pg][...], axis=0)
    m_s[n] = m_blk
    l_s[n] = l_blk
    acc_s[n] = acc
    gate_s[n] = jnp.sum(q * (ksum * (1.0 / MOBA_BLOCK)), axis=-1, keepdims=True)

    @pl.when(n == n_blocks - 1)
    def _():
        nb = m_s.shape[0]
        gates = [gate_s[i] for i in range(nb)]
        s_own = jnp.sum(qs * kn_ref[0], axis=-1, keepdims=True)
        m_tot = s_own
        sels = []
        for i in range(nb):
            rank = jnp.zeros(gates[i].shape, jnp.int32)
            for j in range(nb):
                if j == i:
                    continue
                ahead = (gates[j] > gates[i]) | (gates[j] == gates[i]) if j < i else gates[j] > gates[i]
                rank = rank + ahead.astype(jnp.int32)
            sel = rank < n_sel
            sels.append(sel)
            m_tot = jnp.where(sel, jnp.maximum(m_tot, m_s[i]), m_tot)
        w_own = jnp.exp(s_own - m_tot)
        num = w_own * vn_ref[0]
        den = w_own
        for i in range(nb):
            w = jnp.where(sels[i], jnp.exp(m_s[i] - m_tot), 0.0)
            num = num + w * acc_s[i]
            den = den + w * l_s[i]
        o_ref[0] = num / den


def _moba_decode(q, k_new, v_new, cache_k, cache_v, page_table, layer):
    bd = q.shape[0]
    n_pages = page_table.shape[1]
    page = cache_k.shape[2]
    past_len = n_pages * page
    assert MOBA_BLOCK % page == 0 and past_len % MOBA_BLOCK == 0
    ppb = MOBA_BLOCK // page
    nb = past_len // MOBA_BLOCK
    n_sel = min(MOBA_TOPK, nb + 1)
    h3 = lambda a: a.reshape(bd, ATT_HEADS, HEAD_DIM)
    slopes = _alibi_slopes(ATT_HEADS).reshape(1, ATT_HEADS, 1)
    toff = jnp.broadcast_to(jnp.arange(page, dtype=F32)[:, None, None], (page, ATT_HEADS, 1))
    tok = pl.BlockSpec((1, ATT_HEADS, HEAD_DIM), lambda bi, n, pt: (bi, 0, 0))

    def page_spec(pg):
        return pl.BlockSpec((None, None, page, ATT_HEADS, HEAD_DIM),
                            lambda bi, n, pt: (pt[bi * n_pages + n * ppb + pg], layer, 0, 0, 0))

    kernel = functools.partial(_moba_decode_kernel, pages_per_block=ppb, page=page,
                               past_len=past_len, n_sel=n_sel)
    out = pl.pallas_call(
        kernel,
        out_shape=jax.ShapeDtypeStruct((bd, ATT_HEADS, HEAD_DIM), F32),
        grid_spec=pltpu.PrefetchScalarGridSpec(
            num_scalar_prefetch=1,
            grid=(bd, nb),
            in_specs=[tok, tok, tok,
                      pl.BlockSpec((1, ATT_HEADS, 1), lambda bi, n, pt: (0, 0, 0)),
                      pl.BlockSpec((page, ATT_HEADS, 1), lambda bi, n, pt: (0, 0, 0))]
                     + [page_spec(pg) for pg in range(ppb)] * 2,
            out_specs=tok,
            scratch_shapes=[pltpu.VMEM((nb, ATT_HEADS, 1), F32), pltpu.VMEM((nb, ATT_HEADS, 1), F32),
                            pltpu.VMEM((nb, ATT_HEADS, HEAD_DIM), F32), pltpu.VMEM((nb, ATT_HEADS, 1), F32)],
        ),
        compiler_params=_params("parallel", "arbitrary"),
        name="moba_decode",
    )(page_table.reshape(-1), h3(q), h3(k_new), h3(v_new), slopes, toff,
      *([cache_k] * ppb), *([cache_v] * ppb))
    return out.reshape(bd, ATT_WIDTH)


def _mix_tail(y_att, y_pool, conv_pre, x, cvec_ref, convpw_ref, wout_ref, ln1_ref,
              rw_ref, rbias_ref, alpha, x1_ref, gates_ref):
    c = _layer_norm(conv_pre + cvec_ref[0:1, :], cvec_ref[1:2, :], cvec_ref[2:3, :])
    y_conv = jnp.dot(_silu(c).astype(BF16), convpw_ref[...], preferred_element_type=F32)
    cat = jnp.concatenate([y_att.astype(BF16), y_pool.astype(BF16), y_conv.astype(BF16)], axis=1)
    mix = jnp.dot(cat, wout_ref[...], preferred_element_type=F32)
    x1 = _layer_norm(alpha * x + mix, ln1_ref[0:1, :], ln1_ref[1:2, :])
    x1_ref[...] = x1

    xh = x1.astype(BF16)
    xl = (x1 - xh.astype(F32)).astype(BF16)
    lhs = jnp.concatenate([xh, xl, xh], axis=1)
    logits_t = jnp.dot(lhs, rw_ref[...], preferred_element_type=F32).T
    e_rows = [logits_t[e:e + 1, :] for e in range(N_EXPERTS)]
    mx = functools.reduce(jnp.maximum, e_rows)
    ex = [jnp.exp(r - mx) for r in e_rows]
    inv_den = 1.0 / functools.reduce(jnp.add, ex)
    probs = [r * inv_den for r in ex]
    sel = [probs[e] + rbias_ref[e:e + 1, :] for e in range(N_EXPERTS)]
    gscore = []
    for g in range(N_EXPERT_GROUPS):
        members = sel[g * EXPERTS_PER_GROUP:(g + 1) * EXPERTS_PER_GROUP]
        pair_sums = [members[i] + members[j] for i in range(EXPERTS_PER_GROUP)
                     for j in range(i + 1, EXPERTS_PER_GROUP)]
        gscore.append(functools.reduce(jnp.maximum, pair_sums))
    best_score = functools.reduce(jnp.maximum, gscore)
    taken = jnp.zeros(best_score.shape, jnp.bool_)
    chosen = []
    for g in range(N_EXPERT_GROUPS):
        is_best = (gscore[g] == best_score) & jnp.logical_not(taken)
        taken = taken | is_best
        members = sel[g * EXPERTS_PER_GROUP:(g + 1) * EXPERTS_PER_GROUP]
        for i in range(EXPERTS_PER_GROUP):
            rank = jnp.zeros(best_score.shape, jnp.int32)
            for j in range(EXPERTS_PER_GROUP):
                if j == i:
                    continue
                ahead = (members[j] >= members[i]) if j < i else (members[j] > members[i])
                rank = rank + ahead.astype(jnp.int32)
            chosen.append(is_best & (rank < EXPERT_TOPK))
    picked = [jnp.where(chosen[e], probs[e], 0.0) for e in range(N_EXPERTS)]
    inv_tot = 1.0 / functools.reduce(jnp.add, picked)
    for e in range(N_EXPERTS):
        gates_ref[e:e + 1, :] = picked[e] * inv_tot


def _pool_lane_select(per_window):
    width = per_window[0].shape[-1]
    gdim = width // POOL_GROUPS
    lane = lax.broadcasted_iota(jnp.int32, per_window[0].shape, 1)
    out = per_window[-1]
    for g in range(POOL_GROUPS - 2, -1, -1):
        out = jnp.where(lane < (g + 1) * gdim, per_window[g], out)
    return out


def _mix_prompt_kernel(yatt_ref, up_ref, uph_ref, glu_ref, gluh_ref, x_ref,
                       poolw_ref, pscale_ref, convw_ref, cvec_ref, convpw_ref, wout_ref, ln1_ref,
                       rw_ref, rbias_ref, x1_ref, gates_ref, pext_ref, cext_ref, cpre_ref,
                       *, tt, tiles_per_seq, alpha):
    i = pl.program_id(0)
    first = (i % tiles_per_seq) == 0
    row0 = (i % tiles_per_seq) * tt
    u = up_ref[...]
    pext_ref[0:POOL_HALO, :] = jnp.where(first, 0.0, uph_ref[...])
    pext_ref[POOL_HALO:, :] = u
    win_sums = []
    cur = u
    for back in range(1, max(POOL_WINDOWS)):
        cur = cur + pext_ref[POOL_HALO - back:POOL_HALO - back + tt, :]
        if back + 1 in POOL_WINDOWS:
            win_sums.append(cur)
    pos1 = (row0 + lax.broadcasted_iota(jnp.int32, (tt, 1), 0) + 1).astype(F32)
    pooled = _pool_lane_select([ws / jnp.minimum(pos1, float(wn)) for ws, wn in zip(win_sums, POOL_WINDOWS)])
    y_pool = jnp.dot((pooled - u).astype(BF16), poolw_ref[...], preferred_element_type=F32) * pscale_ref[...]

    cext_ref[0:CONV_HALO, :] = jnp.where(first, 0.0, gluh_ref[...])
    cext_ref[CONV_HALO:, :] = glu_ref[...]
    lead = CONV_HALO - CONV_HIST
    for c0 in range(0, tt, CONV_ROWS):
        acc = jnp.zeros((CONV_ROWS, glu_ref.shape[1]), F32)
        for j in range(CONV_KERNEL):
            acc = acc + cext_ref[c0 + lead + j:c0 + lead + j + CONV_ROWS, :] * convw_ref[j:j + 1, :]
        cpre_ref[c0:c0 + CONV_ROWS, :] = acc

    _mix_tail(yatt_ref[...], y_pool, cpre_ref[...], x_ref[...], cvec_ref, convpw_ref, wout_ref,
              ln1_ref, rw_ref, rbias_ref, alpha, x1_ref, gates_ref)


def _mix_decode_kernel(yatt_ref, up_ref, pstate_ref, glu_ref, cstate_ref, x_ref,
                       poolw_ref, pscale_ref, convw_ref, cvec_ref, convpw_ref, wout_ref, ln1_ref,
                       rw_ref, rbias_ref, x1_ref, gates_ref, *, past_len, alpha):
    u = up_ref[...]
    ps = pstate_ref[...]
    per_window = []
    for wn in POOL_WINDOWS:
        tot = u + jnp.sum(ps[:, POOL_HIST - (wn - 1):, :], axis=1)
        per_window.append(tot / float(min(past_len + 1, wn)))
    pooled = _pool_lane_select(per_window)
    y_pool = jnp.dot((pooled - u).astype(BF16), poolw_ref[...], preferred_element_type=F32) * pscale_ref[...]
    cw = convw_ref[...]
    conv_pre = (jnp.sum(cstate_ref[...] * cw[None, :CONV_HIST, :], axis=1)
                + glu_ref[...] * cw[CONV_HIST:CONV_KERNEL, :])
    _mix_tail(yatt_ref[...], y_pool, conv_pre, x_ref[...], cvec_ref, convpw_ref, wout_ref,
              ln1_ref, rw_ref, rbias_ref, alpha, x1_ref, gates_ref)


def _mix_weights(lw):
    pool_w = lw["pool_w"]
    g, gd, _ = pool_w.shape
    blockdiag = jnp.zeros((g * gd, g * gd), F32)
    for i in range(g):
        blockdiag = blockdiag.at[i * gd:(i + 1) * gd, i * gd:(i + 1) * gd].set(pool_w[i])
    rw = lw["router_w"]
    rw_h = rw.astype(BF16)
    rw_l = (rw - rw_h.astype(F32)).astype(BF16)
    rw3 = jnp.pad(jnp.concatenate([rw_h, rw_h, rw_l], axis=0), ((0, 0), (0, LANES - N_EXPERTS)))
    return dict(
        poolw=blockdiag.astype(BF16),
        pscale=lw["pool_scale"].reshape(1, -1),
        convw=lw["conv_w"],
        cvec=jnp.stack([lw["conv_b"], lw["conv_ln_g"], lw["conv_ln_b"]]),
        convpw=lw["conv_pw"].astype(BF16),
        wout=lw["w_out"].astype(BF16),
        ln1=jnp.stack([lw["ln1_g"], lw["ln1_b"]]),
        rw=rw3,
        rbias=lw["router_bias"].reshape(N_EXPERTS, 1),
    )


_MIX_W_ORDER = ("poolw", "pscale", "convw", "cvec", "convpw", "wout", "ln1", "rw", "rbias")


def _full_spec(a):
    nd = a.ndim
    return pl.BlockSpec(a.shape, lambda *_: (0,) * nd)


def _mix_prompt(y_att, u_pool, glu, x, mw, *, t, alpha):
    m, d = x.shape
    tt = min(TT_MIX, t)
    assert t % tt == 0 and tt % CONV_ROWS == 0 and tt % CONV_HALO == 0
    tiles_per_seq = t // tt
    pw, cw = u_pool.shape[1], glu.shape[1]
    row = lambda width: pl.BlockSpec((tt, width), lambda i: (i, 0))
    halo = lambda rows, width: pl.BlockSpec(
        (rows, width), lambda i: (jnp.maximum(i * (tt // rows) - 1, 0), 0))
    weights = [mw[k] for k in _MIX_W_ORDER]
    kernel = functools.partial(_mix_prompt_kernel, tt=tt, tiles_per_seq=tiles_per_seq, alpha=alpha)
    return pl.pallas_call(
        kernel,
        out_shape=[jax.ShapeDtypeStruct((m, d), F32), jax.ShapeDtypeStruct((N_EXPERTS, m), F32)],
        grid=(m // tt,),
        in_specs=[row(y_att.shape[1]), row(pw), halo(POOL_HALO, pw), row(cw), halo(CONV_HALO, cw), row(d)]
                 + [_full_spec(w) for w in weights],
        out_specs=[row(d), pl.BlockSpec((N_EXPERTS, tt), lambda i: (0, i))],
        scratch_shapes=[pltpu.VMEM((POOL_HALO + tt, pw), F32), pltpu.VMEM((CONV_HALO + tt, cw), F32),
                        pltpu.VMEM((tt, cw), F32)],
        compiler_params=_params("parallel"),
        name="mix_prompt",
    )(y_att, u_pool, u_pool, glu, glu, x, *weights)


def _mix_decode(y_att, u_pool, pool_state, glu, conv_state, x, mw, *, past_len, alpha):
    m, d = x.shape
    weights = [mw[k] for k in _MIX_W_ORDER]
    ins = [y_att, u_pool, pool_state, glu, conv_state, x] + weights
    kernel = functools.partial(_mix_decode_kernel, past_len=past_len, alpha=alpha)
    return pl.pallas_call(
        kernel,
        out_shape=[jax.ShapeDtypeStruct((m, d), F32), jax.ShapeDtypeStruct((N_EXPERTS, m), F32)],
        grid=(1,),
        in_specs=[_full_spec(a) for a in ins],
        out_specs=[pl.BlockSpec((m, d), lambda i: (0, 0)), pl.BlockSpec((N_EXPERTS, m), lambda i: (0, 0))],
        compiler_params=_params("arbitrary"),
        name="mix_decode",
    )(*ins)


def _moe_kernel(x_ref, gates_ref, wg_ref, wu_ref, wd_ref, ln_ref, o_ref, xb_ref, acc_ref, *, alpha):
    e = pl.program_id(1)

    @pl.when(e == 0)
    def _():
        xb_ref[...] = x_ref[...].astype(BF16)
        acc_ref[...] = jnp.zeros(acc_ref.shape, F32)

    xb = xb_ref[...]
    hg = jnp.dot(xb, wg_ref[0].astype(BF16), preferred_element_type=F32)
    hu = jnp.dot(xb, wu_ref[0].astype(BF16), preferred_element_type=F32)
    gates = gates_ref[...]
    lane = lax.broadcasted_iota(jnp.int32, gates.shape, 1)
    gate_e = jnp.sum(jnp.where(lane == e, gates, 0.0), axis=1, keepdims=True)
    hidden = (_silu(hg) * hu * gate_e).astype(BF16)
    acc_ref[...] += jnp.dot(hidden, wd_ref[0].astype(BF16), preferred_element_type=F32)

    @pl.when(e == pl.num_programs(1) - 1)
    def _():
        o_ref[...] = _layer_norm(alpha * x_ref[...] + acc_ref[...], ln_ref[0:1, :], ln_ref[1:2, :])


def _moe(x, gates, exp_gate, exp_up, exp_down, ln2, *, layer, alpha):
    m, d = x.shape
    tm = min(TM_MOE, m)
    assert m % tm == 0
    de = exp_gate.shape[-1]
    return pl.pallas_call(
        functools.partial(_moe_kernel, alpha=alpha),
        out_shape=jax.ShapeDtypeStruct((m, d), F32),
        grid=(m // tm, N_EXPERTS),
        in_specs=[
            pl.BlockSpec((tm, d), lambda i, e: (i, 0)),
            pl.BlockSpec((tm, N_EXPERTS), lambda i, e: (i, 0)),
            pl.BlockSpec((None, 1, d, de), lambda i, e: (layer, e, 0, 0)),
            pl.BlockSpec((None, 1, d, de), lambda i, e: (layer, e, 0, 0)),
            pl.BlockSpec((None, 1, de, d), lambda i, e: (layer, e, 0, 0)),
            pl.BlockSpec((2, d), lambda i, e: (0, 0)),
        ],
        out_specs=pl.BlockSpec((tm, d), lambda i, e: (i, 0)),
        scratch_shapes=[pltpu.VMEM((tm, d), BF16), pltpu.VMEM((tm, d), F32)],
        compiler_params=_params("parallel", "arbitrary"),
        name="moe",
    )(x, gates, exp_gate, exp_up, exp_down, ln2)


def kernel(x_prompt, x_sample, cache_k, cache_v, state_pool, state_conv, page_table, w_in, pool_w, pool_scale, conv_w, conv_b, conv_ln_g, conv_ln_b, conv_pw, w_out, ln1_g, ln1_b, router_w, router_bias, exp_gate, exp_up, exp_down, ln2_g, ln2_b):
    b, t, d = x_prompt.shape
    bd, td, _ = x_sample.shape
    assert td == 1
    depth = w_in.shape[0]
    alpha = (2 * depth) ** 0.25
    past_len = page_table.shape[1] * cache_k.shape[2]

    hp = x_prompt.reshape(b * t, d)
    hs = x_sample.reshape(bd, d)
    kp_l, vp_l, ks_l, vs_l, pp_l, ps_l, cp_l, cs_l = [], [], [], [], [], [], [], []
    for l in range(depth):
        lw = dict(pool_w=pool_w[l], pool_scale=pool_scale[l], conv_w=conv_w[l], conv_b=conv_b[l],
                  conv_ln_g=conv_ln_g[l], conv_ln_b=conv_ln_b[l], conv_pw=conv_pw[l], w_out=w_out[l],
                  ln1_g=ln1_g[l], ln1_b=ln1_b[l], router_w=router_w, router_bias=router_bias)
        mw = _mix_weights(lw)
        w_in_bf = w_in[l].astype(BF16)
        ln2 = jnp.stack([ln2_g[l], ln2_b[l]])

        q, k, v, u_pool, glu, kb, vb, kmean = _in_proj(hp, w_in_bf, tm=min(TM_IN, t), with_aux=True)
        y_att = _moba_prompt(q, kb, vb, kmean, b, t)
        x1, gates_t = _mix_prompt(y_att, u_pool, glu, hp, mw, t=t, alpha=alpha)
        hp = _moe(x1, gates_t.T, exp_gate, exp_up, exp_down, ln2, layer=l, alpha=alpha)
        kp_l.append(k.reshape(b, t, ATT_HEADS, HEAD_DIM))
        vp_l.append(v.reshape(b, t, ATT_HEADS, HEAD_DIM))
        pp_l.append(u_pool.reshape(b, t, -1)[:, t - POOL_HIST:])
        cp_l.append(glu.reshape(b, t, -1)[:, t - CONV_HIST:])

        qs, ks, vs, us, gs = _in_proj(hs, w_in_bf, tm=bd, with_aux=False)
        ys = _moba_decode(qs, ks, vs, cache_k, cache_v, page_table, l)
        x1s, gates_ts = _mix_decode(ys, us, state_pool[:, l], gs, state_conv[:, l], hs, mw,
                                    past_len=past_len, alpha=alpha)
        hs = _moe(x1s, gates_ts.T, exp_gate, exp_up, exp_down, ln2, layer=l, alpha=alpha)
        ks_l.append(ks.reshape(bd, 1, ATT_HEADS, HEAD_DIM))
        vs_l.append(vs.reshape(bd, 1, ATT_HEADS, HEAD_DIM))
        ps_l.append(jnp.concatenate([state_pool[:, l, 1:], us[:, None, :]], axis=1))
        cs_l.append(jnp.concatenate([state_conv[:, l, 1:], gs[:, None, :]], axis=1))

    stack = lambda xs: jnp.stack(xs, axis=1)
    return (hp.reshape(b, t, d), hs.reshape(bd, 1, d), stack(kp_l), stack(vp_l), stack(ks_l), stack(vs_l),
            stack(pp_l), stack(ps_l), stack(cp_l), stack(cs_l))
```

```python
import functools
import math

import jax
import jax.numpy as jnp
from jax import lax
from jax.experimental import pallas as pl
from jax.experimental.pallas import tpu as pltpu

F32 = jnp.float32
BF16 = jnp.bfloat16

ATT_HEADS = 8
HEAD_DIM = 64
ATT_WIDTH = ATT_HEADS * HEAD_DIM
POOL_WINDOWS = (2, 4, 8, 16)
POOL_GROUPS = len(POOL_WINDOWS)
POOL_HIST = max(POOL_WINDOWS) - 1
CONV_KERNEL = 31
CONV_HIST = CONV_KERNEL - 1
MOBA_BLOCK = 256
MOBA_TOPK = 3
N_EXPERTS = 16
N_EXPERT_GROUPS = 4
EXPERTS_PER_GROUP = N_EXPERTS // N_EXPERT_GROUPS
EXPERT_TOPK = 2
LN_EPS = 1e-5
NEG_BIG = -1e30

LOG2_E = math.log2(math.e)
LANES = 128
SUBLANES = 8
HEADS_PER_TILE = LANES // HEAD_DIM
POOL_HALO = 16
CONV_HALO = 32
VMEM_LIMIT = 56 * 1024 * 1024

TM_IN = 512
TT_MIX = 512
TM_MOE = 1024
CONV_ROWS = 64


def _params(*sem):
    return pltpu.CompilerParams(dimension_semantics=sem, vmem_limit_bytes=VMEM_LIMIT)


def _sigmoid(x):
    return 1.0 / (1.0 + jnp.exp(-x))


def _silu(x):
    return x * _sigmoid(x)


def _layer_norm(x, g, b):
    mu = jnp.mean(x, axis=-1, keepdims=True)
    xc = x - mu
    var = jnp.mean(xc * xc, axis=-1, keepdims=True)
    return xc * lax.rsqrt(var + LN_EPS) * g + b


def _alibi_slopes(n_heads):
    return 2.0 ** (-8.0 * jnp.arange(1, n_heads + 1, dtype=F32) / n_heads)


def _proj_segments(xb, w_ref, pw, cw):
    aw = ATT_WIDTH

    def seg(lo, width):
        return jnp.dot(xb, w_ref[:, lo:lo + width], preferred_element_type=F32)

    glu = seg(3 * aw + pw, cw) * _sigmoid(seg(3 * aw + pw + cw, cw))
    return seg, seg(3 * aw, pw), glu


def _in_proj_decode_kernel(x_ref, w_ref, q_ref, k_ref, v_ref, up_ref, glu_ref):
    xb = x_ref[...].astype(BF16)
    seg, up, glu = _proj_segments(xb, w_ref, up_ref.shape[-1], glu_ref.shape[-1])
    q_ref[...] = seg(0, ATT_WIDTH)
    k_ref[...] = seg(ATT_WIDTH, ATT_WIDTH)
    v_ref[...] = seg(2 * ATT_WIDTH, ATT_WIDTH)
    up_ref[...] = up
    glu_ref[...] = glu


def _in_proj_prompt_kernel(x_ref, w_ref, wqkvt_ref, qt_ref, kb_ref, km_ref, vtb_ref, up_ref, glu_ref,
                           kt_ref, vt_ref, *, tm):
    aw = ATT_WIDTH
    blk = MOBA_BLOCK
    xb = x_ref[...].astype(BF16)
    seg, up, glu = _proj_segments(xb, w_ref, up_ref.shape[-1], glu_ref.shape[-1])
    up_ref[...] = up
    glu_ref[...] = glu
    k = seg(aw, aw)
    kb_ref[...] = k.astype(BF16)
    for i in range(tm // blk):
        km_ref[i] = jnp.mean(k[i * blk:(i + 1) * blk], axis=0, keepdims=True)
    qkvt = lax.dot_general(wqkvt_ref[...], xb, (((1,), (1,)), ((), ())), preferred_element_type=F32)
    qt_ref[...] = qkvt[:aw]
    kt_ref[...] = qkvt[aw:2 * aw]
    vt = qkvt[2 * aw:]
    vt_ref[...] = vt
    for i in range(tm // blk):
        vtb_ref[i] = vt[:, i * blk:(i + 1) * blk].astype(BF16)


def _split_widths(n):
    pw = (n - 3 * ATT_WIDTH) // 3
    return pw, pw


def _in_proj_decode(x, w_bf):
    m, d = x.shape
    n = w_bf.shape[1]
    pw, cw = _split_widths(n)
    full = lambda width: pl.BlockSpec((m, width), lambda i: (0, 0))
    return pl.pallas_call(
        _in_proj_decode_kernel,
        out_shape=[jax.ShapeDtypeStruct((m, ATT_WIDTH), F32)] * 3 + [
            jax.ShapeDtypeStruct((m, pw), F32), jax.ShapeDtypeStruct((m, cw), F32)],
        grid=(1,),
        in_specs=[full(d), pl.BlockSpec((d, n), lambda i: (0, 0))],
        out_specs=[full(ATT_WIDTH)] * 3 + [full(pw), full(cw)],
        compiler_params=_params("arbitrary"),
        name="in_proj_decode",
    )(x, w_bf)


def _in_proj_prompt(x, w_bf, wqkvt_bf, *, b, t):
    m, d = x.shape
    n = w_bf.shape[1]
    aw = ATT_WIDTH
    blk = MOBA_BLOCK
    pw, cw = _split_widths(n)
    tm = min(TM_IN, t)
    assert t % tm == 0 and tm % blk == 0
    tps = t // tm
    nblk = tm // blk
    row = lambda width: pl.BlockSpec((tm, width), lambda i: (i, 0))
    tok_lanes = pl.BlockSpec((None, aw, tm), lambda i: (i // tps, 0, i % tps))
    return pl.pallas_call(
        functools.partial(_in_proj_prompt_kernel, tm=tm),
        out_shape=[jax.ShapeDtypeStruct((b, aw, t), F32), jax.ShapeDtypeStruct((m, aw), BF16),
                   jax.ShapeDtypeStruct((m // blk, 1, aw), F32), jax.ShapeDtypeStruct((m // blk, aw, blk), BF16),
                   jax.ShapeDtypeStruct((m, pw), F32), jax.ShapeDtypeStruct((m, cw), F32),
                   jax.ShapeDtypeStruct((b, aw, t), F32), jax.ShapeDtypeStruct((b, aw, t), F32)],
        grid=(m // tm,),
        in_specs=[row(d), pl.BlockSpec((d, n), lambda i: (0, 0)), pl.BlockSpec((3 * aw, d), lambda i: (0, 0))],
        out_specs=[tok_lanes, row(aw), pl.BlockSpec((nblk, 1, aw), lambda i: (i, 0, 0)),
                   pl.BlockSpec((nblk, aw, blk), lambda i: (i, 0, 0)), row(pw), row(cw), tok_lanes, tok_lanes],
        compiler_params=_params("parallel"),
        name="in_proj_prompt",
    )(x, w_bf, wqkvt_bf)


def _moba_prompt_kernel(slopes_ref, q_ref, k_ref, vt_ref, km_ref, ab_ref,
                        o_ref, qt_ref, sel_ref, sa_ref, sb_ref, pa_ref, pb_ref, *, n_sel):
    blk = MOBA_BLOCK
    hp = pl.program_id(1)
    qi = pl.program_id(2)
    q_t = q_ref[...]
    km = km_ref[0]
    nbp = km.shape[0]
    scale = HEAD_DIM ** -0.5 * LOG2_E

    km_lane = lax.broadcasted_iota(jnp.int32, km.shape, 1)
    km_heads = jnp.concatenate(
        [jnp.where((km_lane >= hh * HEAD_DIM) & (km_lane < (hh + 1) * HEAD_DIM), km, 0.0)
         for hh in range(HEADS_PER_TILE)], axis=0)
    gates = jnp.dot(km_heads, q_t, precision=lax.Precision.HIGHEST, preferred_element_type=F32)
    row = lax.broadcasted_iota(jnp.int32, (nbp, blk), 0)
    row_f = row.astype(F32)
    dim = lax.broadcasted_iota(jnp.int32, (LANES, blk), 0)
    for hh in range(HEADS_PER_TILE):
        gate = jnp.where(row < qi, gates[hh * nbp:(hh + 1) * nbp], -jnp.inf)
        sel = jnp.zeros((nbp, blk), jnp.bool_)
        for _ in range(n_sel):
            mx = jnp.max(gate, axis=0, keepdims=True)
            first = jnp.min(jnp.where(gate == mx, row_f, float(nbp)), axis=0, keepdims=True)
            pick = (row_f == first) & (mx > -jnp.inf)
            sel = sel | pick
            gate = jnp.where(pick, -jnp.inf, gate)
        sel_ref[hh] = jnp.where(sel | (row == qi), 1.0, 0.0)
        in_head = (dim >= hh * HEAD_DIM) & (dim < (hh + 1) * HEAD_DIM)
        qt_ref[hh] = (jnp.where(in_head, q_t, 0.0) * scale).astype(BF16)

    slopes = [slopes_ref[hp * HEADS_PER_TILE + hh] for hh in range(HEADS_PER_TILE)]

    def scores(j, slot_ref):
        jc = jnp.minimum(j, qi)
        start = pl.multiple_of(jc * blk, blk)
        kj = k_ref[0, pl.ds(start, blk), :]
        own = (jc == qi).astype(jnp.int32)
        for hh in range(HEADS_PER_TILE):
            slot_ref[hh] = jnp.dot(kj, qt_ref[hh], preferred_element_type=F32) + ab_ref[own, hh]

    def softmax(j, slot_ref, p_ref, ms, ls):
        jc = jnp.minimum(j, qi)
        dist = ((qi - jc) * blk).astype(F32)
        new_ms, new_ls, alphas = [], [], []
        for hh in range(HEADS_PER_TILE):
            s = slot_ref[hh]
            picked = (sel_ref[hh, pl.ds(jc, 1), :] > 0.5) & (j <= qi)
            c = -(slopes[hh] * dist)
            m_new = jnp.maximum(ms[hh], jnp.max(s, axis=0, keepdims=True) + c)
            shift = jnp.where(picked, m_new - c, -NEG_BIG)
            m_new = jnp.where(picked, m_new, ms[hh])
            p = jnp.exp2(s - shift)
            alpha = jnp.exp2(ms[hh] - m_new)
            new_ls.append(alpha * ls[hh] + jnp.sum(p, axis=0, keepdims=True))
            p_ref[hh] = p.astype(BF16)
            new_ms.append(m_new)
            alphas.append(alpha)
        return tuple(new_ms), tuple(new_ls), tuple(alphas)

    def weighted_values(j, p_ref, alphas, accs):
        vtj = vt_ref[jnp.clip(j, 0, qi)]
        return tuple(accs[hh] * alphas[hh]
                     + jnp.dot(vtj[hh * HEAD_DIM:(hh + 1) * HEAD_DIM, :], p_ref[hh], preferred_element_type=F32)
                     for hh in range(HEADS_PER_TILE))

    def pair(jj, carry):
        ms, ls, accs, pending = carry
        j0 = 2 * jj
        scores(j0 + 1, sb_ref)
        accs = weighted_values(j0 - 1, pb_ref, pending, accs)
        ms, ls, alphas = softmax(j0, sa_ref, pa_ref, ms, ls)
        scores(j0 + 2, sa_ref)
        accs = weighted_values(j0, pa_ref, alphas, accs)
        ms, ls, pending = softmax(j0 + 1, sb_ref, pb_ref, ms, ls)
        return ms, ls, accs, pending

    heads = range(HEADS_PER_TILE)
    init = (tuple(jnp.full((1, blk), NEG_BIG, F32) for _ in heads),
            tuple(jnp.zeros((1, blk), F32) for _ in heads),
            tuple(jnp.zeros((HEAD_DIM, blk), F32) for _ in heads),
            tuple(jnp.ones((1, blk), F32) for _ in heads))
    pb_ref[...] = jnp.zeros(pb_ref.shape, BF16)
    scores(jnp.int32(0), sa_ref)
    n_pairs = qi // 2 + 1
    ms, ls, accs, pending = lax.fori_loop(0, n_pairs, pair, init)
    accs = weighted_values(2 * n_pairs - 1, pb_ref, pending, accs)
    out_t = jnp.concatenate([accs[hh] * (1.0 / ls[hh]) for hh in range(HEADS_PER_TILE)], axis=0)
    o_ref[...] = out_t.T.astype(o_ref.dtype)


def _moba_prompt(qt, kb, vtb, kmean, b, t):
    blk = MOBA_BLOCK
    nb = t // blk
    assert t % blk == 0
    nbp = -(-nb // SUBLANES) * SUBLANES
    n_sel = min(MOBA_TOPK, nb)
    slopes = _alibi_slopes(ATT_HEADS) * LOG2_E
    r = jnp.arange(blk, dtype=F32)
    rel = r[None, :] - r[:, None]
    ab_off = -slopes[:, None, None] * rel[None]
    ab = jnp.stack([ab_off, jnp.where(rel[None] >= 0, ab_off, NEG_BIG)])
    km = jnp.pad(kmean.reshape(b, nb, ATT_WIDTH), ((0, 0), (0, nbp - nb), (0, 0)))
    kb3 = kb.reshape(b, t, ATT_WIDTH)
    n_hp = ATT_HEADS // HEADS_PER_TILE
    return pl.pallas_call(
        functools.partial(_moba_prompt_kernel, n_sel=n_sel),
        out_shape=jax.ShapeDtypeStruct((b * t, ATT_WIDTH), BF16),
        grid=(b, n_hp, nb),
        in_specs=[
            pl.BlockSpec(memory_space=pltpu.SMEM),
            pl.BlockSpec((None, LANES, blk), lambda bi, hp, qi: (bi, hp, qi)),
            pl.BlockSpec((1, t, LANES), lambda bi, hp, qi: (bi, 0, hp)),
            pl.BlockSpec((nb, LANES, blk), lambda bi, hp, qi: (bi, hp, 0)),
            pl.BlockSpec((1, nbp, LANES), lambda bi, hp, qi: (bi, 0, hp)),
            pl.BlockSpec((2, HEADS_PER_TILE, blk, blk), lambda bi, hp, qi: (0, hp, 0, 0)),
        ],
        out_specs=pl.BlockSpec((blk, LANES), lambda bi, hp, qi: (bi * nb + qi, hp)),
        scratch_shapes=[pltpu.VMEM((HEADS_PER_TILE, LANES, blk), BF16),
                        pltpu.VMEM((HEADS_PER_TILE, nbp, blk), F32),
                        pltpu.VMEM((HEADS_PER_TILE, blk, blk), F32),
                        pltpu.VMEM((HEADS_PER_TILE, blk, blk), F32),
                        pltpu.VMEM((HEADS_PER_TILE, blk, blk), BF16),
                        pltpu.VMEM((HEADS_PER_TILE, blk, blk), BF16)],
        compiler_params=_params("parallel", "parallel", "arbitrary"),
        name="moba_prompt",
    )(slopes, qt, kb3, vtb, km, ab)


def _lane_bcast_cols(row):
    slabs = [jnp.broadcast_to(row[:, s * LANES:(s + 1) * LANES], (LANES, LANES)).T
             for s in range(ATT_WIDTH // LANES)]
    return jnp.concatenate(slabs, axis=0).reshape(ATT_HEADS, HEAD_DIM, LANES)


def _moba_decode_kernel(pt_ref, q_ref, kn_ref, vn_ref, slopes_ref, *rest,
                        pages_per_block, page, past_len, n_sel):
    kt_refs = rest[:pages_per_block]
    vt_refs = rest[pages_per_block:2 * pages_per_block]
    o_ref = rest[2 * pages_per_block]
    qb_s, m_s, l_s, gate_s, acc_s = rest[2 * pages_per_block + 1:]
    n = pl.program_id(1)
    n_blocks = pl.num_programs(1)
    scale = HEAD_DIM ** -0.5

    @pl.when(n == 0)
    def _():
        qb_s[...] = _lane_bcast_cols(q_ref[0])

    qb = qb_s[...]
    slopes = slopes_ref[...]
    lane_f = lax.broadcasted_iota(jnp.int32, (1, 1, page), 2).astype(F32)
    raws, scores = [], []
    for pg in range(pages_per_block):
        raw = jnp.sum(kt_refs[pg][...] * qb, axis=1, keepdims=True)
        base = (past_len - n * MOBA_BLOCK - pg * page).astype(F32)
        scores.append(raw * scale - slopes * (base - lane_f))
        raws.append(raw)
    m_blk = functools.reduce(jnp.maximum, [jnp.max(s, axis=2, keepdims=True) for s in scores])
    l_blk = jnp.zeros(m_blk.shape, F32)
    accv = jnp.zeros(qb.shape, F32)
    for pg in range(pages_per_block):
        p = jnp.exp(scores[pg] - m_blk)
        l_blk = l_blk + jnp.sum(p, axis=2, keepdims=True)
        accv = accv + p * vt_refs[pg][...]
    m_s[n] = m_blk
    l_s[n] = l_blk
    acc_s[n] = accv
    gate_s[n] = functools.reduce(jnp.add, [jnp.sum(r, axis=2, keepdims=True) for r in raws]) * (1.0 / MOBA_BLOCK)

    @pl.when(n == n_blocks - 1)
    def _():
        nb = m_s.shape[0]
        gates = [gate_s[i] for i in range(nb)]
        knb = _lane_bcast_cols(kn_ref[0])
        vnb = _lane_bcast_cols(vn_ref[0])
        s_own = jnp.max(jnp.sum(knb * qb, axis=1, keepdims=True), axis=2, keepdims=True) * scale
        m_tot = s_own
        sels = []
        for i in range(nb):
            rank = jnp.zeros(gates[i].shape, jnp.int32)
            for j in range(nb):
                if j == i:
                    continue
                ahead = (gates[j] >= gates[i]) if j < i else (gates[j] > gates[i])
                rank = rank + ahead.astype(jnp.int32)
            sel = rank < n_sel
            sels.append(sel)
            m_tot = jnp.where(sel, jnp.maximum(m_tot, m_s[i]), m_tot)
        w_own = jnp.exp(s_own - m_tot)
        lane = lax.broadcasted_iota(jnp.int32, vnb.shape, 2)
        num = jnp.where(lane == 0, w_own * vnb, 0.0)
        den = w_own
        for i in range(nb):
            w = jnp.where(sels[i], jnp.exp(m_s[i] - m_tot), 0.0)
            num = num + w * acc_s[i]
            den = den + w * l_s[i]
        flat = (num * (1.0 / den)).reshape(ATT_WIDTH, LANES)
        pieces = [jnp.sum(flat[s * LANES:(s + 1) * LANES, :].T, axis=0, keepdims=True)
                  for s in range(ATT_WIDTH // LANES)]
        o_ref[0] = jnp.concatenate(pieces, axis=1)


def _moba_decode(q, k_new, v_new, cache_k, cache_v, page_table, layer):
    bd = q.shape[0]
    n_pages = page_table.shape[1]
    page = cache_k.shape[2]
    past_len = n_pages * page
    assert MOBA_BLOCK % page == 0 and past_len % MOBA_BLOCK == 0 and page == LANES
    ppb = MOBA_BLOCK // page
    nb = past_len // MOBA_BLOCK
    n_sel = min(MOBA_TOPK, nb + 1)
    kt = jnp.transpose(cache_k, (0, 1, 3, 4, 2))
    vt = jnp.transpose(cache_v, (0, 1, 3, 4, 2))
    row3 = lambda a: a.reshape(bd, 1, ATT_WIDTH)
    slopes = _alibi_slopes(ATT_HEADS).reshape(ATT_HEADS, 1, 1)
    tok = pl.BlockSpec((1, 1, ATT_WIDTH), lambda bi, n, pt: (bi, 0, 0))

    def page_spec(pg):
        return pl.BlockSpec((None, None, ATT_HEADS, HEAD_DIM, page),
                            lambda bi, n, pt: (pt[bi * n_pages + n * ppb + pg], layer, 0, 0, 0))

    kernel = functools.partial(_moba_decode_kernel, pages_per_block=ppb, page=page,
                               past_len=past_len, n_sel=n_sel)
    stat = pltpu.VMEM((nb, ATT_HEADS, 1, 1), F32)
    out = pl.pallas_call(
        kernel,
        out_shape=jax.ShapeDtypeStruct((bd, 1, ATT_WIDTH), F32),
        grid_spec=pltpu.PrefetchScalarGridSpec(
            num_scalar_prefetch=1,
            grid=(bd, nb),
            in_specs=[tok, tok, tok, pl.BlockSpec((ATT_HEADS, 1, 1), lambda bi, n, pt: (0, 0, 0))]
                     + [page_spec(pg) for pg in range(ppb)] * 2,
            out_specs=tok,
            scratch_shapes=[pltpu.VMEM((ATT_HEADS, HEAD_DIM, page), F32), stat, stat, stat,
                            pltpu.VMEM((nb, ATT_HEADS, HEAD_DIM, page), F32)],
        ),
        compiler_params=_params("parallel", "arbitrary"),
        name="moba_decode",
    )(page_table.reshape(-1), row3(q), row3(k_new), row3(v_new), slopes,
      *([kt] * ppb), *([vt] * ppb))
    return out.reshape(bd, ATT_WIDTH)


def _mix_tail(y_att, y_pool, conv_pre, x, cvec_ref, convpw_ref, wout_ref, ln1_ref,
              rw_ref, rbias_ref, alpha, x1_ref, gates_ref):
    c = _layer_norm(conv_pre + cvec_ref[0:1, :], cvec_ref[1:2, :], cvec_ref[2:3, :])
    y_conv = jnp.dot(_silu(c).astype(BF16), convpw_ref[...], preferred_element_type=F32)
    cat = jnp.concatenate([y_att.astype(BF16), y_pool.astype(BF16), y_conv.astype(BF16)], axis=1)
    mix = jnp.dot(cat, wout_ref[...], preferred_element_type=F32)
    x1 = _layer_norm(alpha * x + mix, ln1_ref[0:1, :], ln1_ref[1:2, :])
    x1_ref[...] = x1

    xh = x1.astype(BF16)
    xl = (x1 - xh.astype(F32)).astype(BF16)
    lhs = jnp.concatenate([xh, xl, xh], axis=1)
    logits_t = jnp.dot(lhs, rw_ref[...], preferred_element_type=F32).T
    e_rows = [logits_t[e:e + 1, :] for e in range(N_EXPERTS)]
    mx = functools.reduce(jnp.maximum, e_rows)
    ex = [jnp.exp(r - mx) for r in e_rows]
    inv_den = 1.0 / functools.reduce(jnp.add, ex)
    probs = [r * inv_den for r in ex]
    sel = [probs[e] + rbias_ref[e:e + 1, :] for e in range(N_EXPERTS)]
    gscore = []
    for g in range(N_EXPERT_GROUPS):
        members = sel[g * EXPERTS_PER_GROUP:(g + 1) * EXPERTS_PER_GROUP]
        pair_sums = [members[i] + members[j] for i in range(EXPERTS_PER_GROUP)
                     for j in range(i + 1, EXPERTS_PER_GROUP)]
        gscore.append(functools.reduce(jnp.maximum, pair_sums))
    best_score = functools.reduce(jnp.maximum, gscore)
    taken = jnp.zeros(best_score.shape, jnp.bool_)
    chosen = []
    for g in range(N_EXPERT_GROUPS):
        is_best = (gscore[g] == best_score) & jnp.logical_not(taken)
        taken = taken | is_best
        members = sel[g * EXPERTS_PER_GROUP:(g + 1) * EXPERTS_PER_GROUP]
        for i in range(EXPERTS_PER_GROUP):
            rank = jnp.zeros(best_score.shape, jnp.int32)
            for j in range(EXPERTS_PER_GROUP):
                if j == i:
                    continue
                ahead = (members[j] >= members[i]) if j < i else (members[j] > members[i])
                rank = rank + ahead.astype(jnp.int32)
            chosen.append(is_best & (rank < EXPERT_TOPK))
    picked = [jnp.where(chosen[e], probs[e], 0.0) for e in range(N_EXPERTS)]
    inv_tot = 1.0 / functools.reduce(jnp.add, picked)
    for e in range(N_EXPERTS):
        gates_ref[e:e + 1, :] = picked[e] * inv_tot


def _pool_lane_select(per_window):
    width = per_window[0].shape[-1]
    gdim = width // POOL_GROUPS
    lane = lax.broadcasted_iota(jnp.int32, per_window[0].shape, 1)
    out = per_window[-1]
    for g in range(POOL_GROUPS - 2, -1, -1):
        out = jnp.where(lane < (g + 1) * gdim, per_window[g], out)
    return out


def _mix_prompt_kernel(yatt_ref, up_ref, uph_ref, glu_ref, gluh_ref, x_ref,
                       poolw_ref, pscale_ref, convw_ref, cvec_ref, convpw_ref, wout_ref, ln1_ref,
                       rw_ref, rbias_ref, x1_ref, gates_ref, pext_ref, cext_ref, cpre_ref,
                       *, tt, tiles_per_seq, alpha):
    i = pl.program_id(0)
    first = (i % tiles_per_seq) == 0
    row0 = (i % tiles_per_seq) * tt
    u = up_ref[...]
    pext_ref[0:POOL_HALO, :] = jnp.where(first, 0.0, uph_ref[...])
    pext_ref[POOL_HALO:, :] = u
    win_sums = []
    cur = u
    for back in range(1, max(POOL_WINDOWS)):
        cur = cur + pext_ref[POOL_HALO - back:POOL_HALO - back + tt, :]
        if back + 1 in POOL_WINDOWS:
            win_sums.append(cur)
    pos1 = (row0 + lax.broadcasted_iota(jnp.int32, (tt, 1), 0) + 1).astype(F32)
    pooled = _pool_lane_select([ws / jnp.minimum(pos1, float(wn)) for ws, wn in zip(win_sums, POOL_WINDOWS)])
    y_pool = jnp.dot((pooled - u).astype(BF16), poolw_ref[...], preferred_element_type=F32) * pscale_ref[...]

    cext_ref[0:CONV_HALO, :] = jnp.where(first, 0.0, gluh_ref[...])
    cext_ref[CONV_HALO:, :] = glu_ref[...]
    lead = CONV_HALO - CONV_HIST
    for c0 in range(0, tt, CONV_ROWS):
        acc = jnp.zeros((CONV_ROWS, glu_ref.shape[1]), F32)
        for j in range(CONV_KERNEL):
            acc = acc + cext_ref[c0 + lead + j:c0 + lead + j + CONV_ROWS, :] * convw_ref[j:j + 1, :]
        cpre_ref[c0:c0 + CONV_ROWS, :] = acc

    _mix_tail(yatt_ref[...], y_pool, cpre_ref[...], x_ref[...], cvec_ref, convpw_ref, wout_ref,
              ln1_ref, rw_ref, rbias_ref, alpha, x1_ref, gates_ref)


def _mix_decode_kernel(yatt_ref, up_ref, pstate_ref, glu_ref, cstate_ref, x_ref,
                       poolw_ref, pscale_ref, convw_ref, cvec_ref, convpw_ref, wout_ref, ln1_ref,
                       rw_ref, rbias_ref, x1_ref, gates_ref, *, past_len, alpha):
    u = up_ref[...]
    ps = pstate_ref[...]
    per_window = []
    for wn in POOL_WINDOWS:
        tot = u + jnp.sum(ps[:, POOL_HIST - (wn - 1):, :], axis=1)
        per_window.append(tot / float(min(past_len + 1, wn)))
    pooled = _pool_lane_select(per_window)
    y_pool = jnp.dot((pooled - u).astype(BF16), poolw_ref[...], preferred_element_type=F32) * pscale_ref[...]
    cw = convw_ref[...]
    conv_pre = (jnp.sum(cstate_ref[...] * cw[None, :CONV_HIST, :], axis=1)
                + glu_ref[...] * cw[CONV_HIST:CONV_KERNEL, :])
    _mix_tail(yatt_ref[...], y_pool, conv_pre, x_ref[...], cvec_ref, convpw_ref, wout_ref,
              ln1_ref, rw_ref, rbias_ref, alpha, x1_ref, gates_ref)


def _mix_weights(lw):
    pool_w = lw["pool_w"]
    g, gd, _ = pool_w.shape
    blockdiag = jnp.zeros((g * gd, g * gd), F32)
    for i in range(g):
        blockdiag = blockdiag.at[i * gd:(i + 1) * gd, i * gd:(i + 1) * gd].set(pool_w[i])
    rw = lw["router_w"]
    rw_h = rw.astype(BF16)
    rw_l = (rw - rw_h.astype(F32)).astype(BF16)
    rw3 = jnp.pad(jnp.concatenate([rw_h, rw_h, rw_l], axis=0), ((0, 0), (0, LANES - N_EXPERTS)))
    return dict(
        poolw=blockdiag.astype(BF16),
        pscale=lw["pool_scale"].reshape(1, -1),
        convw=lw["conv_w"],
        cvec=jnp.stack([lw["conv_b"], lw["conv_ln_g"], lw["conv_ln_b"]]),
        convpw=lw["conv_pw"].astype(BF16),
        wout=lw["w_out"].astype(BF16),
        ln1=jnp.stack([lw["ln1_g"], lw["ln1_b"]]),
        rw=rw3,
        rbias=lw["router_bias"].reshape(N_EXPERTS, 1),
    )


_MIX_W_ORDER = ("poolw", "pscale", "convw", "cvec", "convpw", "wout", "ln1", "rw", "rbias")


def _full_spec(a):
    nd = a.ndim
    return pl.BlockSpec(a.shape, lambda *_: (0,) * nd)


def _mix_prompt(y_att, u_pool, glu, x, mw, *, t, alpha):
    m, d = x.shape
    tt = min(TT_MIX, t)
    assert t % tt == 0 and tt % CONV_ROWS == 0 and tt % CONV_HALO == 0
    tiles_per_seq = t // tt
    pw, cw = u_pool.shape[1], glu.shape[1]
    row = lambda width: pl.BlockSpec((tt, width), lambda i: (i, 0))
    halo = lambda rows, width: pl.BlockSpec(
        (rows, width), lambda i: (jnp.maximum(i * (tt // rows) - 1, 0), 0))
    weights = [mw[k] for k in _MIX_W_ORDER]
    kernel = functools.partial(_mix_prompt_kernel, tt=tt, tiles_per_seq=tiles_per_seq, alpha=alpha)
    return pl.pallas_call(
        kernel,
        out_shape=[jax.ShapeDtypeStruct((m, d), F32), jax.ShapeDtypeStruct((N_EXPERTS, m), F32)],
        grid=(m // tt,),
        in_specs=[row(y_att.shape[1]), row(pw), halo(POOL_HALO, pw), row(cw), halo(CONV_HALO, cw), row(d)]
                 + [_full_spec(w) for w in weights],
        out_specs=[row(d), pl.BlockSpec((N_EXPERTS, tt), lambda i: (0, i))],
        scratch_shapes=[pltpu.VMEM((POOL_HALO + tt, pw), F32), pltpu.VMEM((CONV_HALO + tt, cw), F32),
                        pltpu.VMEM((tt, cw), F32)],
        compiler_params=_params("parallel"),
        name="mix_prompt",
    )(y_att, u_pool, u_pool, glu, glu, x, *weights)


def _mix_decode(y_att, u_pool, pool_state, glu, conv_state, x, mw, *, past_len, alpha):
    m, d = x.shape
    weights = [mw[k] for k in _MIX_W_ORDER]
    ins = [y_att, u_pool, pool_state, glu, conv_state, x] + weights
    kernel = functools.partial(_mix_decode_kernel, past_len=past_len, alpha=alpha)
    return pl.pallas_call(
        kernel,
        out_shape=[jax.ShapeDtypeStruct((m, d), F32), jax.ShapeDtypeStruct((N_EXPERTS, m), F32)],
        grid=(1,),
        in_specs=[_full_spec(a) for a in ins],
        out_specs=[pl.BlockSpec((m, d), lambda i: (0, 0)), pl.BlockSpec((N_EXPERTS, m), lambda i: (0, 0))],
        compiler_params=_params("arbitrary"),
        name="mix_decode",
    )(*ins)


def _moe_kernel(x_ref, gates_ref, wg_ref, wu_ref, wd_ref, ln_ref, o_ref, xb_ref, acc_ref, *, alpha):
    e = pl.program_id(1)

    @pl.when(e == 0)
    def _():
        xb_ref[...] = x_ref[...].astype(BF16)
        acc_ref[...] = jnp.zeros(acc_ref.shape, F32)

    xb = xb_ref[...]
    hg = jnp.dot(xb, wg_ref[0].astype(BF16), preferred_element_type=F32)
    hu = jnp.dot(xb, wu_ref[0].astype(BF16), preferred_element_type=F32)
    gates = gates_ref[...]
    lane = lax.broadcasted_iota(jnp.int32, gates.shape, 1)
    gate_e = jnp.sum(jnp.where(lane == e, gates, 0.0), axis=1, keepdims=True)
    hidden = (_silu(hg) * hu * gate_e).astype(BF16)
    acc_ref[...] += jnp.dot(hidden, wd_ref[0].astype(BF16), preferred_element_type=F32)

    @pl.when(e == pl.num_programs(1) - 1)
    def _():
        o_ref[...] = _layer_norm(alpha * x_ref[...] + acc_ref[...], ln_ref[0:1, :], ln_ref[1:2, :])


def _moe(x, gates, exp_gate, exp_up, exp_down, ln2, *, layer, alpha):
    m, d = x.shape
    tm = min(TM_MOE, m)
    assert m % tm == 0
    de = exp_gate.shape[-1]
    return pl.pallas_call(
        functools.partial(_moe_kernel, alpha=alpha),
        out_shape=jax.ShapeDtypeStruct((m, d), F32),
        grid=(m // tm, N_EXPERTS),
        in_specs=[
            pl.BlockSpec((tm, d), lambda i, e: (i, 0)),
            pl.BlockSpec((tm, N_EXPERTS), lambda i, e: (i, 0)),
            pl.BlockSpec((None, 1, d, de), lambda i, e: (layer, e, 0, 0)),
            pl.BlockSpec((None, 1, d, de), lambda i, e: (layer, e, 0, 0)),
            pl.BlockSpec((None, 1, de, d), lambda i, e: (layer, e, 0, 0)),
            pl.BlockSpec((2, d), lambda i, e: (0, 0)),
        ],
        out_specs=pl.BlockSpec((tm, d), lambda i, e: (i, 0)),
        scratch_shapes=[pltpu.VMEM((tm, d), BF16), pltpu.VMEM((tm, d), F32)],
        compiler_params=_params("parallel", "arbitrary"),
        name="moe",
    )(x, gates, exp_gate, exp_up, exp_down, ln2)


def kernel(x_prompt, x_sample, cache_k, cache_v, state_pool, state_conv, page_table, w_in, pool_w, pool_scale, conv_w, conv_b, conv_ln_g, conv_ln_b, conv_pw, w_out, ln1_g, ln1_b, router_w, router_bias, exp_gate, exp_up, exp_down, ln2_g, ln2_b):
    b, t, d = x_prompt.shape
    bd, td, _ = x_sample.shape
    assert td == 1
    depth = w_in.shape[0]
    alpha = (2 * depth) ** 0.25
    past_len = page_table.shape[1] * cache_k.shape[2]

    hp = x_prompt.reshape(b * t, d)
    hs = x_sample.reshape(bd, d)
    kp_l, vp_l, ks_l, vs_l, pp_l, ps_l, cp_l, cs_l = [], [], [], [], [], [], [], []
    for l in range(depth):
        lw = dict(pool_w=pool_w[l], pool_scale=pool_scale[l], conv_w=conv_w[l], conv_b=conv_b[l],
                  conv_ln_g=conv_ln_g[l], conv_ln_b=conv_ln_b[l], conv_pw=conv_pw[l], w_out=w_out[l],
                  ln1_g=ln1_g[l], ln1_b=ln1_b[l], router_w=router_w, router_bias=router_bias)
        mw = _mix_weights(lw)
        w_in_bf = w_in[l].astype(BF16)
        wqkvt_bf = w_in_bf[:, :3 * ATT_WIDTH].T
        ln2 = jnp.stack([ln2_g[l], ln2_b[l]])

        qt, kb, kmean, vtb, u_pool, glu, kt, vt = _in_proj_prompt(hp, w_in_bf, wqkvt_bf, b=b, t=t)
        y_att = _moba_prompt(qt, kb, vtb, kmean, b, t)
        x1, gates_t = _mix_prompt(y_att, u_pool, glu, hp, mw, t=t, alpha=alpha)
        hp = _moe(x1, gates_t.T, exp_gate, exp_up, exp_down, ln2, layer=l, alpha=alpha)
        kp_l.append(kt)
        vp_l.append(vt)
        pp_l.append(u_pool.reshape(b, t, -1)[:, t - POOL_HIST:])
        cp_l.append(glu.reshape(b, t, -1)[:, t - CONV_HIST:])

        qs, ks, vs, us, gs = _in_proj_decode(hs, w_in_bf)
        ys = _moba_decode(qs, ks, vs, cache_k, cache_v, page_table, l)
        x1s, gates_ts = _mix_decode(ys, us, state_pool[:, l], gs, state_conv[:, l], hs, mw,
                                    past_len=past_len, alpha=alpha)
        hs = _moe(x1s, gates_ts.T, exp_gate, exp_up, exp_down, ln2, layer=l, alpha=alpha)
        ks_l.append(ks.reshape(bd, 1, ATT_HEADS, HEAD_DIM))
        vs_l.append(vs.reshape(bd, 1, ATT_HEADS, HEAD_DIM))
        ps_l.append(jnp.concatenate([state_pool[:, l, 1:], us[:, None, :]], axis=1))
        cs_l.append(jnp.concatenate([state_conv[:, l, 1:], gs[:, None, :]], axis=1))

    stack = lambda xs: jnp.stack(xs, axis=1)

    def heads_last(xs):
        return jnp.transpose(stack(xs).reshape(b, depth, ATT_HEADS, HEAD_DIM, t), (0, 1, 4, 2, 3))

    return (hp.reshape(b, t, d), hs.reshape(bd, 1, d), heads_last(kp_l), heads_last(vp_l), stack(ks_l), stack(vs_l),
            stack(pp_l), stack(ps_l), stack(cp_l), stack(cs_l))
```

```python
import functools
import math

import jax
import jax.numpy as jnp
from jax import lax
from jax.experimental import pallas as pl
from jax.experimental.pallas import tpu as pltpu

F32 = jnp.float32
BF16 = jnp.bfloat16

ATT_HEADS = 8
HEAD_DIM = 64
ATT_WIDTH = ATT_HEADS * HEAD_DIM
POOL_WINDOWS = (2, 4, 8, 16)
POOL_GROUPS = len(POOL_WINDOWS)
POOL_HIST = max(POOL_WINDOWS) - 1
CONV_KERNEL = 31
CONV_HIST = CONV_KERNEL - 1
MOBA_BLOCK = 256
MOBA_TOPK = 3
N_EXPERTS = 16
N_EXPERT_GROUPS = 4
EXPERTS_PER_GROUP = N_EXPERTS // N_EXPERT_GROUPS
EXPERT_TOPK = 2
LN_EPS = 1e-5
NEG_BIG = -1e30

LOG2_E = math.log2(math.e)
LANES = 128
SUBLANES = 8
HEADS_PER_TILE = LANES // HEAD_DIM
POOL_HALO = 16
CONV_HALO = 32
VMEM_LIMIT = 56 * 1024 * 1024

TM_IN = 512
TT_MIX = 512
TM_MOE = 1024
CONV_ROWS = 64


def _params(*sem):
    return pltpu.CompilerParams(dimension_semantics=sem, vmem_limit_bytes=VMEM_LIMIT)


def _sigmoid(x):
    return 1.0 / (1.0 + jnp.exp(-x))


def _silu(x):
    return x * _sigmoid(x)


def _layer_norm(x, g, b):
    mu = jnp.mean(x, axis=-1, keepdims=True)
    xc = x - mu
    var = jnp.mean(xc * xc, axis=-1, keepdims=True)
    return xc * lax.rsqrt(var + LN_EPS) * g + b


def _alibi_slopes(n_heads):
    return 2.0 ** (-8.0 * jnp.arange(1, n_heads + 1, dtype=F32) / n_heads)


def _proj_segments(xb, w_ref, pw, cw):
    aw = ATT_WIDTH

    def seg(lo, width):
        return jnp.dot(xb, w_ref[:, lo:lo + width], preferred_element_type=F32)

    glu = seg(3 * aw + pw, cw) * _sigmoid(seg(3 * aw + pw + cw, cw))
    return seg, seg(3 * aw, pw), glu


def _in_proj_decode_kernel(x_ref, w_ref, q_ref, k_ref, v_ref, up_ref, glu_ref):
    xb = x_ref[...].astype(BF16)
    seg, up, glu = _proj_segments(xb, w_ref, up_ref.shape[-1], glu_ref.shape[-1])
    q_ref[...] = seg(0, ATT_WIDTH)
    k_ref[...] = seg(ATT_WIDTH, ATT_WIDTH)
    v_ref[...] = seg(2 * ATT_WIDTH, ATT_WIDTH)
    up_ref[...] = up
    glu_ref[...] = glu


def _in_proj_prompt_kernel(x_ref, w_ref, wqkvt_ref, qt_ref, kb_ref, km_ref, vtb_ref, up_ref, glu_ref,
                           kt_ref, vt_ref, *, tm):
    aw = ATT_WIDTH
    blk = MOBA_BLOCK
    xb = x_ref[...].astype(BF16)
    seg, up, glu = _proj_segments(xb, w_ref, up_ref.shape[-1], glu_ref.shape[-1])
    up_ref[...] = up
    glu_ref[...] = glu
    k = seg(aw, aw)
    kb_ref[...] = k.astype(BF16)
    for i in range(tm // blk):
        km_ref[i] = jnp.mean(k[i * blk:(i + 1) * blk], axis=0, keepdims=True)
    qkvt = lax.dot_general(wqkvt_ref[...], xb, (((1,), (1,)), ((), ())), preferred_element_type=F32)
    qt_ref[...] = qkvt[:aw]
    kt_ref[...] = qkvt[aw:2 * aw]
    vt = qkvt[2 * aw:]
    vt_ref[...] = vt
    for i in range(tm // blk):
        vtb_ref[i] = vt[:, i * blk:(i + 1) * blk].astype(BF16)


def _split_widths(n):
    pw = (n - 3 * ATT_WIDTH) // 3
    return pw, pw


def _in_proj_decode(x, w_bf):
    m, d = x.shape
    n = w_bf.shape[1]
    pw, cw = _split_widths(n)
    full = lambda width: pl.BlockSpec((m, width), lambda i: (0, 0))
    return pl.pallas_call(
        _in_proj_decode_kernel,
        out_shape=[jax.ShapeDtypeStruct((m, ATT_WIDTH), F32)] * 3 + [
            jax.ShapeDtypeStruct((m, pw), F32), jax.ShapeDtypeStruct((m, cw), F32)],
        grid=(1,),
        in_specs=[full(d), pl.BlockSpec((d, n), lambda i: (0, 0))],
        out_specs=[full(ATT_WIDTH)] * 3 + [full(pw), full(cw)],
        compiler_params=_params("arbitrary"),
        name="in_proj_decode",
    )(x, w_bf)


def _in_proj_prompt(x, w_bf, wqkvt_bf, *, b, t):
    m, d = x.shape
    n = w_bf.shape[1]
    aw = ATT_WIDTH
    blk = MOBA_BLOCK
    pw, cw = _split_widths(n)
    tm = min(TM_IN, t)
    assert t % tm == 0 and tm % blk == 0
    tps = t // tm
    nblk = tm // blk
    row = lambda width: pl.BlockSpec((tm, width), lambda i: (i, 0))
    tok_lanes = pl.BlockSpec((None, aw, tm), lambda i: (i // tps, 0, i % tps))
    return pl.pallas_call(
        functools.partial(_in_proj_prompt_kernel, tm=tm),
        out_shape=[jax.ShapeDtypeStruct((b, aw, t), F32), jax.ShapeDtypeStruct((m, aw), BF16),
                   jax.ShapeDtypeStruct((m // blk, 1, aw), F32), jax.ShapeDtypeStruct((m // blk, aw, blk), BF16),
                   jax.ShapeDtypeStruct((m, pw), F32), jax.ShapeDtypeStruct((m, cw), F32),
                   jax.ShapeDtypeStruct((b, aw, t), F32), jax.ShapeDtypeStruct((b, aw, t), F32)],
        grid=(m // tm,),
        in_specs=[row(d), pl.BlockSpec((d, n), lambda i: (0, 0)), pl.BlockSpec((3 * aw, d), lambda i: (0, 0))],
        out_specs=[tok_lanes, row(aw), pl.BlockSpec((nblk, 1, aw), lambda i: (i, 0, 0)),
                   pl.BlockSpec((nblk, aw, blk), lambda i: (i, 0, 0)), row(pw), row(cw), tok_lanes, tok_lanes],
        compiler_params=_params("parallel"),
        name="in_proj_prompt",
    )(x, w_bf, wqkvt_bf)


def _moba_prompt_kernel(slopes_ref, q_ref, k_ref, vt_ref, km_ref, ab_ref,
                        o_ref, qt_ref, sel_ref, sa_ref, sb_ref, pa_ref, pb_ref, *, n_sel):
    blk = MOBA_BLOCK
    hp = pl.program_id(1)
    qi = pl.program_id(2)
    q_t = q_ref[...]
    km = km_ref[0]
    nbp = km.shape[0]
    scale = HEAD_DIM ** -0.5 * LOG2_E

    km_lane = lax.broadcasted_iota(jnp.int32, km.shape, 1)
    km_heads = jnp.concatenate(
        [jnp.where((km_lane >= hh * HEAD_DIM) & (km_lane < (hh + 1) * HEAD_DIM), km, 0.0)
         for hh in range(HEADS_PER_TILE)], axis=0)
    gates = jnp.dot(km_heads, q_t, precision=lax.Precision.HIGHEST, preferred_element_type=F32)
    row = lax.broadcasted_iota(jnp.int32, (nbp, blk), 0)
    row_f = row.astype(F32)
    dim = lax.broadcasted_iota(jnp.int32, (LANES, blk), 0)
    for hh in range(HEADS_PER_TILE):
        gate = jnp.where(row < qi, gates[hh * nbp:(hh + 1) * nbp], -jnp.inf)
        sel = jnp.zeros((nbp, blk), jnp.bool_)
        for _ in range(n_sel):
            mx = jnp.max(gate, axis=0, keepdims=True)
            first = jnp.min(jnp.where(gate == mx, row_f, float(nbp)), axis=0, keepdims=True)
            pick = (row_f == first) & (mx > -jnp.inf)
            sel = sel | pick
            gate = jnp.where(pick, -jnp.inf, gate)
        sel_ref[hh] = jnp.where(sel | (row == qi), 1.0, 0.0)
        in_head = (dim >= hh * HEAD_DIM) & (dim < (hh + 1) * HEAD_DIM)
        qt_ref[hh] = (jnp.where(in_head, q_t, 0.0) * scale).astype(BF16)

    slopes = [slopes_ref[hp * HEADS_PER_TILE + hh] for hh in range(HEADS_PER_TILE)]

    def scores(j, slot_ref):
        jc = jnp.minimum(j, qi)
        start = pl.multiple_of(jc * blk, blk)
        kj = k_ref[0, pl.ds(start, blk), :]
        own = (jc == qi).astype(jnp.int32)
        for hh in range(HEADS_PER_TILE):
            slot_ref[hh] = jnp.dot(kj, qt_ref[hh], preferred_element_type=F32) + ab_ref[own, hh]

    def softmax(j, slot_ref, p_ref, ms, ls):
        jc = jnp.minimum(j, qi)
        dist = ((qi - jc) * blk).astype(F32)
        new_ms, new_ls, alphas = [], [], []
        for hh in range(HEADS_PER_TILE):
            s = slot_ref[hh]
            picked = (sel_ref[hh, pl.ds(jc, 1), :] > 0.5) & (j <= qi)
            c = -(slopes[hh] * dist)
            m_new = jnp.maximum(ms[hh], jnp.max(s, axis=0, keepdims=True) + c)
            shift = jnp.where(picked, m_new - c, -NEG_BIG)
            m_new = jnp.where(picked, m_new, ms[hh])
            p = jnp.exp2(s - shift)
            alpha = jnp.exp2(ms[hh] - m_new)
            new_ls.append(alpha * ls[hh] + jnp.sum(p, axis=0, keepdims=True))
            p_ref[hh] = p.astype(BF16)
            new_ms.append(m_new)
            alphas.append(alpha)
        return tuple(new_ms), tuple(new_ls), tuple(alphas)

    def weighted_values(j, p_ref, alphas, accs):
        vtj = vt_ref[jnp.clip(j, 0, qi)]
        return tuple(accs[hh] * alphas[hh]
                     + jnp.dot(vtj[hh * HEAD_DIM:(hh + 1) * HEAD_DIM, :], p_ref[hh], preferred_element_type=F32)
                     for hh in range(HEADS_PER_TILE))

    def pair(jj, carry):
        ms, ls, accs, pending = carry
        j0 = 2 * jj
        scores(j0 + 1, sb_ref)
        accs = weighted_values(j0 - 1, pb_ref, pending, accs)
        ms, ls, alphas = softmax(j0, sa_ref, pa_ref, ms, ls)
        scores(j0 + 2, sa_ref)
        accs = weighted_values(j0, pa_ref, alphas, accs)
        ms, ls, pending = softmax(j0 + 1, sb_ref, pb_ref, ms, ls)
        return ms, ls, accs, pending

    heads = range(HEADS_PER_TILE)
    init = (tuple(jnp.full((1, blk), NEG_BIG, F32) for _ in heads),
            tuple(jnp.zeros((1, blk), F32) for _ in heads),
            tuple(jnp.zeros((HEAD_DIM, blk), F32) for _ in heads),
            tuple(jnp.ones((1, blk), F32) for _ in heads))
    pb_ref[...] = jnp.zeros(pb_ref.shape, BF16)
    scores(jnp.int32(0), sa_ref)
    n_pairs = qi // 2 + 1
    ms, ls, accs, pending = lax.fori_loop(0, n_pairs, pair, init)
    accs = weighted_values(2 * n_pairs - 1, pb_ref, pending, accs)
    out_t = jnp.concatenate([accs[hh] * (1.0 / ls[hh]) for hh in range(HEADS_PER_TILE)], axis=0)
    o_ref[...] = out_t.T.astype(o_ref.dtype)


def _moba_prompt(qt, kb, vtb, kmean, b, t):
    blk = MOBA_BLOCK
    nb = t // blk
    assert t % blk == 0
    nbp = -(-nb // SUBLANES) * SUBLANES
    n_sel = min(MOBA_TOPK, nb)
    slopes = _alibi_slopes(ATT_HEADS) * LOG2_E
    r = jnp.arange(blk, dtype=F32)
    rel = r[None, :] - r[:, None]
    ab_off = -slopes[:, None, None] * rel[None]
    ab = jnp.stack([ab_off, jnp.where(rel[None] >= 0, ab_off, NEG_BIG)])
    km = jnp.pad(kmean.reshape(b, nb, ATT_WIDTH), ((0, 0), (0, nbp - nb), (0, 0)))
    kb3 = kb.reshape(b, t, ATT_WIDTH)
    n_hp = ATT_HEADS // HEADS_PER_TILE
    return pl.pallas_call(
        functools.partial(_moba_prompt_kernel, n_sel=n_sel),
        out_shape=jax.ShapeDtypeStruct((b * t, ATT_WIDTH), BF16),
        grid=(b, n_hp, nb),
        in_specs=[
            pl.BlockSpec(memory_space=pltpu.SMEM),
            pl.BlockSpec((None, LANES, blk), lambda bi, hp, qi: (bi, hp, qi)),
            pl.BlockSpec((1, t, LANES), lambda bi, hp, qi: (bi, 0, hp)),
            pl.BlockSpec((nb, LANES, blk), lambda bi, hp, qi: (bi, hp, 0)),
            pl.BlockSpec((1, nbp, LANES), lambda bi, hp, qi: (bi, 0, hp)),
            pl.BlockSpec((2, HEADS_PER_TILE, blk, blk), lambda bi, hp, qi: (0, hp, 0, 0)),
        ],
        out_specs=pl.BlockSpec((blk, LANES), lambda bi, hp, qi: (bi * nb + qi, hp)),
        scratch_shapes=[pltpu.VMEM((HEADS_PER_TILE, LANES, blk), BF16),
                        pltpu.VMEM((HEADS_PER_TILE, nbp, blk), F32),
                        pltpu.VMEM((HEADS_PER_TILE, blk, blk), F32),
                        pltpu.VMEM((HEADS_PER_TILE, blk, blk), F32),
                        pltpu.VMEM((HEADS_PER_TILE, blk, blk), BF16),
                        pltpu.VMEM((HEADS_PER_TILE, blk, blk), BF16)],
        compiler_params=_params("parallel", "parallel", "arbitrary"),
        name="moba_prompt",
    )(slopes, qt, kb3, vtb, km, ab)


def _lane_bcast_cols(row):
    slabs = [jnp.broadcast_to(row[:, s * LANES:(s + 1) * LANES], (LANES, LANES)).T
             for s in range(ATT_WIDTH // LANES)]
    return jnp.concatenate(slabs, axis=0).reshape(ATT_HEADS, HEAD_DIM, LANES)


def _moba_decode_kernel(pt_ref, q_ref, kn_ref, vn_ref, slopes_ref, *rest,
                        pages_per_block, page, past_len, n_sel):
    n_pages = (len(rest) - 1) // 2
    kt_refs = rest[:n_pages]
    vt_refs = rest[n_pages:2 * n_pages]
    o_ref = rest[2 * n_pages]
    nb = n_pages // pages_per_block
    scale = HEAD_DIM ** -0.5
    qb = _lane_bcast_cols(q_ref[0])
    slopes = slopes_ref[...]
    lane_f = lax.broadcasted_iota(jnp.int32, (1, 1, page), 2).astype(F32)

    scores, gates, blk_max = [], [], []
    for i in range(nb):
        raw_sum, top = None, None
        for pg in range(i * pages_per_block, (i + 1) * pages_per_block):
            raw = jnp.sum(kt_refs[pg][...] * qb, axis=1, keepdims=True)
            s = raw * scale - slopes * (float(past_len - pg * page) - lane_f)
            scores.append(s)
            raw_sum = raw if raw_sum is None else raw_sum + raw
            top = s if top is None else jnp.maximum(top, s)
        gates.append(jnp.sum(raw_sum, axis=2, keepdims=True) * (1.0 / MOBA_BLOCK))
        blk_max.append(jnp.max(top, axis=2, keepdims=True))

    knb = _lane_bcast_cols(kn_ref[0])
    vnb = _lane_bcast_cols(vn_ref[0])
    s_own = jnp.max(jnp.sum(knb * qb, axis=1, keepdims=True), axis=2, keepdims=True) * scale
    m_tot = s_own
    sels = []
    for i in range(nb):
        rank = jnp.zeros(gates[i].shape, jnp.int32)
        for j in range(nb):
            if j == i:
                continue
            ahead = (gates[j] >= gates[i]) if j < i else (gates[j] > gates[i])
            rank = rank + ahead.astype(jnp.int32)
        sel = rank < n_sel
        sels.append(sel)
        m_tot = jnp.where(sel, jnp.maximum(m_tot, blk_max[i]), m_tot)

    w_own = jnp.exp(s_own - m_tot)
    lane = lax.broadcasted_iota(jnp.int32, vnb.shape, 2)
    num = jnp.where(lane == 0, w_own * vnb, 0.0)
    den_lanes = jnp.zeros(scores[0].shape, F32)
    for pg in range(n_pages):
        w = jnp.where(sels[pg // pages_per_block], jnp.exp(scores[pg] - m_tot), 0.0)
        den_lanes = den_lanes + w
        num = num + w * vt_refs[pg][...]
    den = w_own + jnp.sum(den_lanes, axis=2, keepdims=True)
    flat = (num * (1.0 / den)).reshape(ATT_WIDTH, LANES)
    pieces = [jnp.sum(flat[s * LANES:(s + 1) * LANES, :].T, axis=0, keepdims=True)
              for s in range(ATT_WIDTH // LANES)]
    o_ref[0] = jnp.concatenate(pieces, axis=1)


def _moba_decode(q, k_new, v_new, cache_k, cache_v, page_table, layer):
    bd = q.shape[0]
    n_pages = page_table.shape[1]
    page = cache_k.shape[2]
    past_len = n_pages * page
    assert MOBA_BLOCK % page == 0 and past_len % MOBA_BLOCK == 0 and page == LANES
    ppb = MOBA_BLOCK // page
    nb = past_len // MOBA_BLOCK
    n_sel = min(MOBA_TOPK, nb + 1)
    kt = jnp.transpose(cache_k, (0, 1, 3, 4, 2))
    vt = jnp.transpose(cache_v, (0, 1, 3, 4, 2))
    row3 = lambda a: a.reshape(bd, 1, ATT_WIDTH)
    slopes = _alibi_slopes(ATT_HEADS).reshape(ATT_HEADS, 1, 1)
    tok = pl.BlockSpec((1, 1, ATT_WIDTH), lambda bi, pt: (bi, 0, 0))

    def page_spec(pg):
        return pl.BlockSpec((None, None, ATT_HEADS, HEAD_DIM, page),
                            lambda bi, pt: (pt[bi * n_pages + pg], layer, 0, 0, 0))

    kernel = functools.partial(_moba_decode_kernel, pages_per_block=ppb, page=page,
                               past_len=past_len, n_sel=n_sel)
    out = pl.pallas_call(
        kernel,
        out_shape=jax.ShapeDtypeStruct((bd, 1, ATT_WIDTH), F32),
        grid_spec=pltpu.PrefetchScalarGridSpec(
            num_scalar_prefetch=1,
            grid=(bd,),
            in_specs=[tok, tok, tok, pl.BlockSpec((ATT_HEADS, 1, 1), lambda bi, pt: (0, 0, 0))]
                     + [page_spec(pg) for pg in range(n_pages)] * 2,
            out_specs=tok,
        ),
        compiler_params=_params("parallel"),
        name="moba_decode",
    )(page_table.reshape(-1), row3(q), row3(k_new), row3(v_new), slopes,
      *([kt] * n_pages), *([vt] * n_pages))
    return out.reshape(bd, ATT_WIDTH)


def _mix_tail(y_att, y_pool, conv_pre, x, cvec_ref, convpw_ref, wout_ref, ln1_ref,
              rw_ref, rbias_ref, alpha, x1_ref, gates_ref):
    c = _layer_norm(conv_pre + cvec_ref[0:1, :], cvec_ref[1:2, :], cvec_ref[2:3, :])
    y_conv = jnp.dot(_silu(c).astype(BF16), convpw_ref[...], preferred_element_type=F32)
    cat = jnp.concatenate([y_att.astype(BF16), y_pool.astype(BF16), y_conv.astype(BF16)], axis=1)
    mix = jnp.dot(cat, wout_ref[...], preferred_element_type=F32)
    x1 = _layer_norm(alpha * x + mix, ln1_ref[0:1, :], ln1_ref[1:2, :])
    x1_ref[...] = x1

    xh = x1.astype(BF16)
    xl = (x1 - xh.astype(F32)).astype(BF16)
    lhs = jnp.concatenate([xh, xl, xh], axis=1)
    logits_t = jnp.dot(lhs, rw_ref[...], preferred_element_type=F32).T
    e_rows = [logits_t[e:e + 1, :] for e in range(N_EXPERTS)]
    mx = functools.reduce(jnp.maximum, e_rows)
    ex = [jnp.exp(r - mx) for r in e_rows]
    inv_den = 1.0 / functools.reduce(jnp.add, ex)
    probs = [r * inv_den for r in ex]
    sel = [probs[e] + rbias_ref[e:e + 1, :] for e in range(N_EXPERTS)]
    gscore = []
    for g in range(N_EXPERT_GROUPS):
        members = sel[g * EXPERTS_PER_GROUP:(g + 1) * EXPERTS_PER_GROUP]
        pair_sums = [members[i] + members[j] for i in range(EXPERTS_PER_GROUP)
                     for j in range(i + 1, EXPERTS_PER_GROUP)]
        gscore.append(functools.reduce(jnp.maximum, pair_sums))
    best_score = functools.reduce(jnp.maximum, gscore)
    taken = jnp.zeros(best_score.shape, jnp.bool_)
    chosen = []
    for g in range(N_EXPERT_GROUPS):
        is_best = (gscore[g] == best_score) & jnp.logical_not(taken)
        taken = taken | is_best
        members = sel[g * EXPERTS_PER_GROUP:(g + 1) * EXPERTS_PER_GROUP]
        for i in range(EXPERTS_PER_GROUP):
            rank = jnp.zeros(best_score.shape, jnp.int32)
            for j in range(EXPERTS_PER_GROUP):
                if j == i:
                    continue
                ahead = (members[j] >= members[i]) if j < i else (members[j] > members[i])
                rank = rank + ahead.astype(jnp.int32)
            chosen.append(is_best & (rank < EXPERT_TOPK))
    picked = [jnp.where(chosen[e], probs[e], 0.0) for e in range(N_EXPERTS)]
    inv_tot = 1.0 / functools.reduce(jnp.add, picked)
    for e in range(N_EXPERTS):
        gates_ref[e:e + 1, :] = picked[e] * inv_tot


def _pool_lane_select(per_window):
    width = per_window[0].shape[-1]
    gdim = width // POOL_GROUPS
    lane = lax.broadcasted_iota(jnp.int32, per_window[0].shape, 1)
    out = per_window[-1]
    for g in range(POOL_GROUPS - 2, -1, -1):
        out = jnp.where(lane < (g + 1) * gdim, per_window[g], out)
    return out


def _mix_prompt_kernel(yatt_ref, up_ref, uph_ref, glu_ref, gluh_ref, x_ref,
                       poolw_ref, pscale_ref, convw_ref, cvec_ref, convpw_ref, wout_ref, ln1_ref,
                       rw_ref, rbias_ref, x1_ref, gates_ref, pext_ref, cext_ref, cpre_ref,
                       *, tt, tiles_per_seq, alpha):
    i = pl.program_id(0)
    first = (i % tiles_per_seq) == 0
    row0 = (i % tiles_per_seq) * tt
    u = up_ref[...]
    pext_ref[0:POOL_HALO, :] = jnp.where(first, 0.0, uph_ref[...])
    pext_ref[POOL_HALO:, :] = u
    win_sums = []
    cur = u
    for back in range(1, max(POOL_WINDOWS)):
        cur = cur + pext_ref[POOL_HALO - back:POOL_HALO - back + tt, :]
        if back + 1 in POOL_WINDOWS:
            win_sums.append(cur)
    pos1 = (row0 + lax.broadcasted_iota(jnp.int32, (tt, 1), 0) + 1).astype(F32)
    pooled = _pool_lane_select([ws / jnp.minimum(pos1, float(wn)) for ws, wn in zip(win_sums, POOL_WINDOWS)])
    y_pool = jnp.dot((pooled - u).astype(BF16), poolw_ref[...], preferred_element_type=F32) * pscale_ref[...]

    cext_ref[0:CONV_HALO, :] = jnp.where(first, 0.0, gluh_ref[...])
    cext_ref[CONV_HALO:, :] = glu_ref[...]
    lead = CONV_HALO - CONV_HIST
    for c0 in range(0, tt, CONV_ROWS):
        acc = jnp.zeros((CONV_ROWS, glu_ref.shape[1]), F32)
        for j in range(CONV_KERNEL):
            acc = acc + cext_ref[c0 + lead + j:c0 + lead + j + CONV_ROWS, :] * convw_ref[j:j + 1, :]
        cpre_ref[c0:c0 + CONV_ROWS, :] = acc

    _mix_tail(yatt_ref[...], y_pool, cpre_ref[...], x_ref[...], cvec_ref, convpw_ref, wout_ref,
              ln1_ref, rw_ref, rbias_ref, alpha, x1_ref, gates_ref)


def _mix_decode_kernel(yatt_ref, up_ref, pstate_ref, glu_ref, cstate_ref, x_ref,
                       poolw_ref, pscale_ref, convw_ref, cvec_ref, convpw_ref, wout_ref, ln1_ref,
                       rw_ref, rbias_ref, x1_ref, gates_ref, *, past_len, alpha):
    u = up_ref[...]
    ps = pstate_ref[...]
    per_window = []
    for wn in POOL_WINDOWS:
        tot = u + jnp.sum(ps[:, POOL_HIST - (wn - 1):, :], axis=1)
        per_window.append(tot / float(min(past_len + 1, wn)))
    pooled = _pool_lane_select(per_window)
    y_pool = jnp.dot((pooled - u).astype(BF16), poolw_ref[...], preferred_element_type=F32) * pscale_ref[...]
    cw = convw_ref[...]
    conv_pre = (jnp.sum(cstate_ref[...] * cw[None, :CONV_HIST, :], axis=1)
                + glu_ref[...] * cw[CONV_HIST:CONV_KERNEL, :])
    _mix_tail(yatt_ref[...], y_pool, conv_pre, x_ref[...], cvec_ref, convpw_ref, wout_ref,
              ln1_ref, rw_ref, rbias_ref, alpha, x1_ref, gates_ref)


def _mix_weights(lw):
    pool_w = lw["pool_w"]
    g, gd, _ = pool_w.shape
    blockdiag = jnp.zeros((g * gd, g * gd), F32)
    for i in range(g):
        blockdiag = blockdiag.at[i * gd:(i + 1) * gd, i * gd:(i + 1) * gd].set(pool_w[i])
    rw = lw["router_w"]
    rw_h = rw.astype(BF16)
    rw_l = (rw - rw_h.astype(F32)).astype(BF16)
    rw3 = jnp.pad(jnp.concatenate([rw_h, rw_h, rw_l], axis=0), ((0, 0), (0, LANES - N_EXPERTS)))
    return dict(
        poolw=blockdiag.astype(BF16),
        pscale=lw["pool_scale"].reshape(1, -1),
        convw=lw["conv_w"],
        cvec=jnp.stack([lw["conv_b"], lw["conv_ln_g"], lw["conv_ln_b"]]),
        convpw=lw["conv_pw"].astype(BF16),
        wout=lw["w_out"].astype(BF16),
        ln1=jnp.stack([lw["ln1_g"], lw["ln1_b"]]),
        rw=rw3,
        rbias=lw["router_bias"].reshape(N_EXPERTS, 1),
    )


_MIX_W_ORDER = ("poolw", "pscale", "convw", "cvec", "convpw", "wout", "ln1", "rw", "rbias")


def _full_spec(a):
    nd = a.ndim
    return pl.BlockSpec(a.shape, lambda *_: (0,) * nd)


def _mix_prompt(y_att, u_pool, glu, x, mw, *, t, alpha):
    m, d = x.shape
    tt = min(TT_MIX, t)
    assert t % tt == 0 and tt % CONV_ROWS == 0 and tt % CONV_HALO == 0
    tiles_per_seq = t // tt
    pw, cw = u_pool.shape[1], glu.shape[1]
    row = lambda width: pl.BlockSpec((tt, width), lambda i: (i, 0))
    halo = lambda rows, width: pl.BlockSpec(
        (rows, width), lambda i: (jnp.maximum(i * (tt // rows) - 1, 0), 0))
    weights = [mw[k] for k in _MIX_W_ORDER]
    kernel = functools.partial(_mix_prompt_kernel, tt=tt, tiles_per_seq=tiles_per_seq, alpha=alpha)
    return pl.pallas_call(
        kernel,
        out_shape=[jax.ShapeDtypeStruct((m, d), F32), jax.ShapeDtypeStruct((N_EXPERTS, m), F32)],
        grid=(m // tt,),
        in_specs=[row(y_att.shape[1]), row(pw), halo(POOL_HALO, pw), row(cw), halo(CONV_HALO, cw), row(d)]
                 + [_full_spec(w) for w in weights],
        out_specs=[row(d), pl.BlockSpec((N_EXPERTS, tt), lambda i: (0, i))],
        scratch_shapes=[pltpu.VMEM((POOL_HALO + tt, pw), F32), pltpu.VMEM((CONV_HALO + tt, cw), F32),
                        pltpu.VMEM((tt, cw), F32)],
        compiler_params=_params("parallel"),
        name="mix_prompt",
    )(y_att, u_pool, u_pool, glu, glu, x, *weights)


def _mix_decode(y_att, u_pool, pool_state, glu, conv_state, x, mw, *, past_len, alpha):
    m, d = x.shape
    weights = [mw[k] for k in _MIX_W_ORDER]
    ins = [y_att, u_pool, pool_state, glu, conv_state, x] + weights
    kernel = functools.partial(_mix_decode_kernel, past_len=past_len, alpha=alpha)
    return pl.pallas_call(
        kernel,
        out_shape=[jax.ShapeDtypeStruct((m, d), F32), jax.ShapeDtypeStruct((N_EXPERTS, m), F32)],
        grid=(1,),
        in_specs=[_full_spec(a) for a in ins],
        out_specs=[pl.BlockSpec((m, d), lambda i: (0, 0)), pl.BlockSpec((N_EXPERTS, m), lambda i: (0, 0))],
        compiler_params=_params("arbitrary"),
        name="mix_decode",
    )(*ins)


def _moe_kernel(x_ref, gates_ref, wg_ref, wu_ref, wd_ref, ln_ref, o_ref, xb_ref, acc_ref, *, alpha):
    e = pl.program_id(1)

    @pl.when(e == 0)
    def _():
        xb_ref[...] = x_ref[...].astype(BF16)
        acc_ref[...] = jnp.zeros(acc_ref.shape, F32)

    xb = xb_ref[...]
    hg = jnp.dot(xb, wg_ref[0].astype(BF16), preferred_element_type=F32)
    hu = jnp.dot(xb, wu_ref[0].astype(BF16), preferred_element_type=F32)
    gates = gates_ref[...]
    lane = lax.broadcasted_iota(jnp.int32, gates.shape, 1)
    gate_e = jnp.sum(jnp.where(lane == e, gates, 0.0), axis=1, keepdims=True)
    hidden = (_silu(hg) * hu * gate_e).astype(BF16)
    acc_ref[...] += jnp.dot(hidden, wd_ref[0].astype(BF16), preferred_element_type=F32)

    @pl.when(e == pl.num_programs(1) - 1)
    def _():
        o_ref[...] = _layer_norm(alpha * x_ref[...] + acc_ref[...], ln_ref[0:1, :], ln_ref[1:2, :])


def _moe(x, gates, exp_gate, exp_up, exp_down, ln2, *, layer, alpha):
    m, d = x.shape
    tm = min(TM_MOE, m)
    assert m % tm == 0
    de = exp_gate.shape[-1]
    return pl.pallas_call(
        functools.partial(_moe_kernel, alpha=alpha),
        out_shape=jax.ShapeDtypeStruct((m, d), F32),
        grid=(m // tm, N_EXPERTS),
        in_specs=[
            pl.BlockSpec((tm, d), lambda i, e: (i, 0)),
            pl.BlockSpec((tm, N_EXPERTS), lambda i, e: (i, 0)),
            pl.BlockSpec((None, 1, d, de), lambda i, e: (layer, e, 0, 0)),
            pl.BlockSpec((None, 1, d, de), lambda i, e: (layer, e, 0, 0)),
            pl.BlockSpec((None, 1, de, d), lambda i, e: (layer, e, 0, 0)),
            pl.BlockSpec((2, d), lambda i, e: (0, 0)),
        ],
        out_specs=pl.BlockSpec((tm, d), lambda i, e: (i, 0)),
        scratch_shapes=[pltpu.VMEM((tm, d), BF16), pltpu.VMEM((tm, d), F32)],
        compiler_params=_params("parallel", "arbitrary"),
        name="moe",
    )(x, gates, exp_gate, exp_up, exp_down, ln2)


def kernel(x_prompt, x_sample, cache_k, cache_v, state_pool, state_conv, page_table, w_in, pool_w, pool_scale, conv_w, conv_b, conv_ln_g, conv_ln_b, conv_pw, w_out, ln1_g, ln1_b, router_w, router_bias, exp_gate, exp_up, exp_down, ln2_g, ln2_b):
    b, t, d = x_prompt.shape
    bd, td, _ = x_sample.shape
    assert td == 1
    depth = w_in.shape[0]
    alpha = (2 * depth) ** 0.25
    past_len = page_table.shape[1] * cache_k.shape[2]

    hp = x_prompt.reshape(b * t, d)
    hs = x_sample.reshape(bd, d)
    kp_l, vp_l, ks_l, vs_l, pp_l, ps_l, cp_l, cs_l = [], [], [], [], [], [], [], []
    for l in range(depth):
        lw = dict(pool_w=pool_w[l], pool_scale=pool_scale[l], conv_w=conv_w[l], conv_b=conv_b[l],
                  conv_ln_g=conv_ln_g[l], conv_ln_b=conv_ln_b[l], conv_pw=conv_pw[l], w_out=w_out[l],
                  ln1_g=ln1_g[l], ln1_b=ln1_b[l], router_w=router_w, router_bias=router_bias)
        mw = _mix_weights(lw)
        w_in_bf = w_in[l].astype(BF16)
        wqkvt_bf = w_in_bf[:, :3 * ATT_WIDTH].T
        ln2 = jnp.stack([ln2_g[l], ln2_b[l]])

        qt, kb, kmean, vtb, u_pool, glu, kt, vt = _in_proj_prompt(hp, w_in_bf, wqkvt_bf, b=b, t=t)
        y_att = _moba_prompt(qt, kb, vtb, kmean, b, t)
        x1, gates_t = _mix_prompt(y_att, u_pool, glu, hp, mw, t=t, alpha=alpha)
        hp = _moe(x1, gates_t.T, exp_gate, exp_up, exp_down, ln2, layer=l, alpha=alpha)
        kp_l.append(kt)
        vp_l.append(vt)
        pp_l.append(u_pool.reshape(b, t, -1)[:, t - POOL_HIST:])
        cp_l.append(glu.reshape(b, t, -1)[:, t - CONV_HIST:])

        qs, ks, vs, us, gs = _in_proj_decode(hs, w_in_bf)
        ys = _moba_decode(qs, ks, vs, cache_k, cache_v, page_table, l)
        x1s, gates_ts = _mix_decode(ys, us, state_pool[:, l], gs, state_conv[:, l], hs, mw,
                                    past_len=past_len, alpha=alpha)
        hs = _moe(x1s, gates_ts.T, exp_gate, exp_up, exp_down, ln2, layer=l, alpha=alpha)
        ks_l.append(ks.reshape(bd, 1, ATT_HEADS, HEAD_DIM))
        vs_l.append(vs.reshape(bd, 1, ATT_HEADS, HEAD_DIM))
        ps_l.append(jnp.concatenate([state_pool[:, l, 1:], us[:, None, :]], axis=1))
        cs_l.append(jnp.concatenate([state_conv[:, l, 1:], gs[:, None, :]], axis=1))

    stack = lambda xs: jnp.stack(xs, axis=1)

    def heads_last(xs):
        return jnp.transpose(stack(xs).reshape(b, depth, ATT_HEADS, HEAD_DIM, t), (0, 1, 4, 2, 3))

    return (hp.reshape(b, t, d), hs.reshape(bd, 1, d), heads_last(kp_l), heads_last(vp_l), stack(ks_l), stack(vs_l),
            stack(pp_l), stack(ps_l), stack(cp_l), stack(cs_l))
```

```python
import functools
import math

import jax
import jax.numpy as jnp
from jax import lax
from jax.experimental import pallas as pl
from jax.experimental.pallas import tpu as pltpu

F32 = jnp.float32
BF16 = jnp.bfloat16

ATT_HEADS = 8
HEAD_DIM = 64
ATT_WIDTH = ATT_HEADS * HEAD_DIM
POOL_WINDOWS = (2, 4, 8, 16)
POOL_GROUPS = len(POOL_WINDOWS)
POOL_HIST = max(POOL_WINDOWS) - 1
CONV_KERNEL = 31
CONV_HIST = CONV_KERNEL - 1
MOBA_BLOCK = 256
MOBA_TOPK = 3
N_EXPERTS = 16
N_EXPERT_GROUPS = 4
EXPERTS_PER_GROUP = N_EXPERTS // N_EXPERT_GROUPS
EXPERT_TOPK = 2
LN_EPS = 1e-5
NEG_BIG = -1e30

LOG2_E = math.log2(math.e)
LANES = 128
SUBLANES = 8
HEADS_PER_TILE = LANES // HEAD_DIM
POOL_HALO = 16
CONV_HALO = 32
VMEM_LIMIT = 56 * 1024 * 1024

TM_IN = 512
TT_MIX = 512
TM_MOE = 1024
TM_MOE_GROUPED = 2048
MOE_CHUNK = 256
VMEM_LIMIT_MOE = 60 * 1024 * 1024
CONV_ROWS = 64


def _params(*sem):
    return pltpu.CompilerParams(dimension_semantics=sem, vmem_limit_bytes=VMEM_LIMIT)


def _sigmoid(x):
    return 1.0 / (1.0 + jnp.exp(-x))


def _silu(x):
    return x * _sigmoid(x)


def _layer_norm(x, g, b):
    mu = jnp.mean(x, axis=-1, keepdims=True)
    xc = x - mu
    var = jnp.mean(xc * xc, axis=-1, keepdims=True)
    return xc * lax.rsqrt(var + LN_EPS) * g + b


def _alibi_slopes(n_heads):
    return 2.0 ** (-8.0 * jnp.arange(1, n_heads + 1, dtype=F32) / n_heads)


def _proj_segments(xb, w_ref, pw, cw):
    aw = ATT_WIDTH

    def seg(lo, width):
        return jnp.dot(xb, w_ref[:, lo:lo + width], preferred_element_type=F32)

    glu = seg(3 * aw + pw, cw) * _sigmoid(seg(3 * aw + pw + cw, cw))
    return seg, seg(3 * aw, pw), glu


def _in_proj_decode_kernel(x_ref, w_ref, q_ref, k_ref, v_ref, up_ref, glu_ref):
    xb = x_ref[...].astype(BF16)
    seg, up, glu = _proj_segments(xb, w_ref, up_ref.shape[-1], glu_ref.shape[-1])
    q_ref[...] = seg(0, ATT_WIDTH)
    k_ref[...] = seg(ATT_WIDTH, ATT_WIDTH)
    v_ref[...] = seg(2 * ATT_WIDTH, ATT_WIDTH)
    up_ref[...] = up
    glu_ref[...] = glu


def _in_proj_prompt_kernel(x_ref, w_ref, wqkvt_ref, qt_ref, kb_ref, km_ref, vtb_ref, up_ref, glu_ref,
                           kt_ref, vt_ref, *, tm):
    aw = ATT_WIDTH
    blk = MOBA_BLOCK
    xb = x_ref[...].astype(BF16)
    seg, up, glu = _proj_segments(xb, w_ref, up_ref.shape[-1], glu_ref.shape[-1])
    up_ref[...] = up
    glu_ref[...] = glu
    k = seg(aw, aw)
    kb_ref[...] = k.astype(BF16)
    for i in range(tm // blk):
        km_ref[i] = jnp.mean(k[i * blk:(i + 1) * blk], axis=0, keepdims=True)
    qkvt = lax.dot_general(wqkvt_ref[...], xb, (((1,), (1,)), ((), ())), preferred_element_type=F32)
    qt_ref[...] = qkvt[:aw]
    kt_ref[...] = qkvt[aw:2 * aw]
    vt = qkvt[2 * aw:]
    vt_ref[...] = vt
    for i in range(tm // blk):
        vtb_ref[i] = vt[:, i * blk:(i + 1) * blk].astype(BF16)


def _split_widths(n):
    pw = (n - 3 * ATT_WIDTH) // 3
    return pw, pw


def _in_proj_decode(x, w_bf):
    m, d = x.shape
    n = w_bf.shape[1]
    pw, cw = _split_widths(n)
    full = lambda width: pl.BlockSpec((m, width), lambda i: (0, 0))
    return pl.pallas_call(
        _in_proj_decode_kernel,
        out_shape=[jax.ShapeDtypeStruct((m, ATT_WIDTH), F32)] * 3 + [
            jax.ShapeDtypeStruct((m, pw), F32), jax.ShapeDtypeStruct((m, cw), F32)],
        grid=(1,),
        in_specs=[full(d), pl.BlockSpec((d, n), lambda i: (0, 0))],
        out_specs=[full(ATT_WIDTH)] * 3 + [full(pw), full(cw)],
        compiler_params=_params("arbitrary"),
        name="in_proj_decode",
    )(x, w_bf)


def _in_proj_prompt(x, w_bf, wqkvt_bf, *, b, t):
    m, d = x.shape
    n = w_bf.shape[1]
    aw = ATT_WIDTH
    blk = MOBA_BLOCK
    pw, cw = _split_widths(n)
    tm = min(TM_IN, t)
    assert t % tm == 0 and tm % blk == 0
    tps = t // tm
    nblk = tm // blk
    row = lambda width: pl.BlockSpec((tm, width), lambda i: (i, 0))
    tok_lanes = pl.BlockSpec((None, aw, tm), lambda i: (i // tps, 0, i % tps))
    return pl.pallas_call(
        functools.partial(_in_proj_prompt_kernel, tm=tm),
        out_shape=[jax.ShapeDtypeStruct((b, aw, t), F32), jax.ShapeDtypeStruct((m, aw), BF16),
                   jax.ShapeDtypeStruct((m // blk, 1, aw), F32), jax.ShapeDtypeStruct((m // blk, aw, blk), BF16),
                   jax.ShapeDtypeStruct((m, pw), F32), jax.ShapeDtypeStruct((m, cw), F32),
                   jax.ShapeDtypeStruct((b, aw, t), F32), jax.ShapeDtypeStruct((b, aw, t), F32)],
        grid=(m // tm,),
        in_specs=[row(d), pl.BlockSpec((d, n), lambda i: (0, 0)), pl.BlockSpec((3 * aw, d), lambda i: (0, 0))],
        out_specs=[tok_lanes, row(aw), pl.BlockSpec((nblk, 1, aw), lambda i: (i, 0, 0)),
                   pl.BlockSpec((nblk, aw, blk), lambda i: (i, 0, 0)), row(pw), row(cw), tok_lanes, tok_lanes],
        compiler_params=_params("parallel"),
        name="in_proj_prompt",
    )(x, w_bf, wqkvt_bf)


def _moba_prompt_kernel(slopes_ref, q_ref, k_ref, vt_ref, km_ref, ab_ref,
                        o_ref, qt_ref, sel_ref, sa_ref, sb_ref, pa_ref, pb_ref, *, n_sel):
    blk = MOBA_BLOCK
    hp = pl.program_id(1)
    qi = pl.program_id(2)
    q_t = q_ref[...]
    km = km_ref[0]
    nbp = km.shape[0]
    scale = HEAD_DIM ** -0.5 * LOG2_E

    km_lane = lax.broadcasted_iota(jnp.int32, km.shape, 1)
    km_heads = jnp.concatenate(
        [jnp.where((km_lane >= hh * HEAD_DIM) & (km_lane < (hh + 1) * HEAD_DIM), km, 0.0)
         for hh in range(HEADS_PER_TILE)], axis=0)
    gates = jnp.dot(km_heads, q_t, precision=lax.Precision.HIGHEST, preferred_element_type=F32)
    row = lax.broadcasted_iota(jnp.int32, (nbp, blk), 0)
    row_f = row.astype(F32)
    dim = lax.broadcasted_iota(jnp.int32, (LANES, blk), 0)
    for hh in range(HEADS_PER_TILE):
        gate = jnp.where(row < qi, gates[hh * nbp:(hh + 1) * nbp], -jnp.inf)
        sel = jnp.zeros((nbp, blk), jnp.bool_)
        for _ in range(n_sel):
            mx = jnp.max(gate, axis=0, keepdims=True)
            first = jnp.min(jnp.where(gate == mx, row_f, float(nbp)), axis=0, keepdims=True)
            pick = (row_f == first) & (mx > -jnp.inf)
            sel = sel | pick
            gate = jnp.where(pick, -jnp.inf, gate)
        sel_ref[hh] = jnp.where(sel | (row == qi), 1.0, 0.0)
        in_head = (dim >= hh * HEAD_DIM) & (dim < (hh + 1) * HEAD_DIM)
        qt_ref[hh] = (jnp.where(in_head, q_t, 0.0) * scale).astype(BF16)

    slopes = [slopes_ref[hp * HEADS_PER_TILE + hh] for hh in range(HEADS_PER_TILE)]

    def scores(j, slot_ref):
        jc = jnp.minimum(j, qi)
        start = pl.multiple_of(jc * blk, blk)
        kj = k_ref[0, pl.ds(start, blk), :]
        own = (jc == qi).astype(jnp.int32)
        for hh in range(HEADS_PER_TILE):
            slot_ref[hh] = jnp.dot(kj, qt_ref[hh], preferred_element_type=F32) + ab_ref[own, hh]

    def softmax(j, slot_ref, p_ref, ms, ls):
        jc = jnp.minimum(j, qi)
        dist = ((qi - jc) * blk).astype(F32)
        new_ms, new_ls, alphas = [], [], []
        for hh in range(HEADS_PER_TILE):
            s = slot_ref[hh]
            picked = (sel_ref[hh, pl.ds(jc, 1), :] > 0.5) & (j <= qi)
            c = -(slopes[hh] * dist)
            m_new = jnp.maximum(ms[hh], jnp.max(s, axis=0, keepdims=True) + c)
            shift = jnp.where(picked, m_new - c, -NEG_BIG)
            m_new = jnp.where(picked, m_new, ms[hh])
            p = jnp.exp2(s - shift)
            alpha = jnp.exp2(ms[hh] - m_new)
            new_ls.append(alpha * ls[hh] + jnp.sum(p, axis=0, keepdims=True))
            p_ref[hh] = p.astype(BF16)
            new_ms.append(m_new)
            alphas.append(alpha)
        return tuple(new_ms), tuple(new_ls), tuple(alphas)

    def weighted_values(j, p_ref, alphas, accs):
        vtj = vt_ref[jnp.clip(j, 0, qi)]
        return tuple(accs[hh] * alphas[hh]
                     + jnp.dot(vtj[hh * HEAD_DIM:(hh + 1) * HEAD_DIM, :], p_ref[hh], preferred_element_type=F32)
                     for hh in range(HEADS_PER_TILE))

    def pair(jj, carry):
        ms, ls, accs, pending = carry
        j0 = 2 * jj
        scores(j0 + 1, sb_ref)
        accs = weighted_values(j0 - 1, pb_ref, pending, accs)
        ms, ls, alphas = softmax(j0, sa_ref, pa_ref, ms, ls)
        scores(j0 + 2, sa_ref)
        accs = weighted_values(j0, pa_ref, alphas, accs)
        ms, ls, pending = softmax(j0 + 1, sb_ref, pb_ref, ms, ls)
        return ms, ls, accs, pending

    heads = range(HEADS_PER_TILE)
    init = (tuple(jnp.full((1, blk), NEG_BIG, F32) for _ in heads),
            tuple(jnp.zeros((1, blk), F32) for _ in heads),
            tuple(jnp.zeros((HEAD_DIM, blk), F32) for _ in heads),
            tuple(jnp.ones((1, blk), F32) for _ in heads))
    pb_ref[...] = jnp.zeros(pb_ref.shape, BF16)
    scores(jnp.int32(0), sa_ref)
    n_pairs = qi // 2 + 1
    ms, ls, accs, pending = lax.fori_loop(0, n_pairs, pair, init)
    accs = weighted_values(2 * n_pairs - 1, pb_ref, pending, accs)
    out_t = jnp.concatenate([accs[hh] * (1.0 / ls[hh]) for hh in range(HEADS_PER_TILE)], axis=0)
    o_ref[...] = out_t.T.astype(o_ref.dtype)


def _moba_prompt(qt, kb, vtb, kmean, b, t):
    blk = MOBA_BLOCK
    nb = t // blk
    assert t % blk == 0
    nbp = -(-nb // SUBLANES) * SUBLANES
    n_sel = min(MOBA_TOPK, nb)
    slopes = _alibi_slopes(ATT_HEADS) * LOG2_E
    r = jnp.arange(blk, dtype=F32)
    rel = r[None, :] - r[:, None]
    ab_off = -slopes[:, None, None] * rel[None]
    ab = jnp.stack([ab_off, jnp.where(rel[None] >= 0, ab_off, NEG_BIG)])
    km = jnp.pad(kmean.reshape(b, nb, ATT_WIDTH), ((0, 0), (0, nbp - nb), (0, 0)))
    kb3 = kb.reshape(b, t, ATT_WIDTH)
    n_hp = ATT_HEADS // HEADS_PER_TILE
    return pl.pallas_call(
        functools.partial(_moba_prompt_kernel, n_sel=n_sel),
        out_shape=jax.ShapeDtypeStruct((b * t, ATT_WIDTH), BF16),
        grid=(b, n_hp, nb),
        in_specs=[
            pl.BlockSpec(memory_space=pltpu.SMEM),
            pl.BlockSpec((None, LANES, blk), lambda bi, hp, qi: (bi, hp, qi)),
            pl.BlockSpec((1, t, LANES), lambda bi, hp, qi: (bi, 0, hp)),
            pl.BlockSpec((nb, LANES, blk), lambda bi, hp, qi: (bi, hp, 0)),
            pl.BlockSpec((1, nbp, LANES), lambda bi, hp, qi: (bi, 0, hp)),
            pl.BlockSpec((2, HEADS_PER_TILE, blk, blk), lambda bi, hp, qi: (0, hp, 0, 0)),
        ],
        out_specs=pl.BlockSpec((blk, LANES), lambda bi, hp, qi: (bi * nb + qi, hp)),
        scratch_shapes=[pltpu.VMEM((HEADS_PER_TILE, LANES, blk), BF16),
                        pltpu.VMEM((HEADS_PER_TILE, nbp, blk), F32),
                        pltpu.VMEM((HEADS_PER_TILE, blk, blk), F32),
                        pltpu.VMEM((HEADS_PER_TILE, blk, blk), F32),
                        pltpu.VMEM((HEADS_PER_TILE, blk, blk), BF16),
                        pltpu.VMEM((HEADS_PER_TILE, blk, blk), BF16)],
        compiler_params=_params("parallel", "parallel", "arbitrary"),
        name="moba_prompt",
    )(slopes, qt, kb3, vtb, km, ab)


def _lane_bcast_cols(row):
    slabs = [jnp.broadcast_to(row[:, s * LANES:(s + 1) * LANES], (LANES, LANES)).T
             for s in range(ATT_WIDTH // LANES)]
    return jnp.concatenate(slabs, axis=0).reshape(ATT_HEADS, HEAD_DIM, LANES)


def _moba_decode_kernel(pt_ref, q_ref, kn_ref, vn_ref, slopes_ref, *rest,
                        pages_per_block, page, past_len, n_sel):
    n_pages = (len(rest) - 1) // 2
    kt_refs = rest[:n_pages]
    vt_refs = rest[n_pages:2 * n_pages]
    o_ref = rest[2 * n_pages]
    nb = n_pages // pages_per_block
    scale = HEAD_DIM ** -0.5
    qb = _lane_bcast_cols(q_ref[0])
    slopes = slopes_ref[...]
    lane_f = lax.broadcasted_iota(jnp.int32, (1, 1, page), 2).astype(F32)

    scores, gates, blk_max = [], [], []
    for i in range(nb):
        raw_sum, top = None, None
        for pg in range(i * pages_per_block, (i + 1) * pages_per_block):
            raw = jnp.sum(kt_refs[pg][...] * qb, axis=1, keepdims=True)
            s = raw * scale - slopes * (float(past_len - pg * page) - lane_f)
            scores.append(s)
            raw_sum = raw if raw_sum is None else raw_sum + raw
            top = s if top is None else jnp.maximum(top, s)
        gates.append(jnp.sum(raw_sum, axis=2, keepdims=True) * (1.0 / MOBA_BLOCK))
        blk_max.append(jnp.max(top, axis=2, keepdims=True))

    knb = _lane_bcast_cols(kn_ref[0])
    vnb = _lane_bcast_cols(vn_ref[0])
    s_own = jnp.max(jnp.sum(knb * qb, axis=1, keepdims=True), axis=2, keepdims=True) * scale
    m_tot = s_own
    sels = []
    for i in range(nb):
        rank = jnp.zeros(gates[i].shape, jnp.int32)
        for j in range(nb):
            if j == i:
                continue
            ahead = (gates[j] >= gates[i]) if j < i else (gates[j] > gates[i])
            rank = rank + ahead.astype(jnp.int32)
        sel = rank < n_sel
        sels.append(sel)
        m_tot = jnp.where(sel, jnp.maximum(m_tot, blk_max[i]), m_tot)

    w_own = jnp.exp(s_own - m_tot)
    lane = lax.broadcasted_iota(jnp.int32, vnb.shape, 2)
    num = jnp.where(lane == 0, w_own * vnb, 0.0)
    den_lanes = jnp.zeros(scores[0].shape, F32)
    for pg in range(n_pages):
        w = jnp.where(sels[pg // pages_per_block], jnp.exp(scores[pg] - m_tot), 0.0)
        den_lanes = den_lanes + w
        num = num + w * vt_refs[pg][...]
    den = w_own + jnp.sum(den_lanes, axis=2, keepdims=True)
    flat = (num * (1.0 / den)).reshape(ATT_WIDTH, LANES)
    pieces = [jnp.sum(flat[s * LANES:(s + 1) * LANES, :].T, axis=0, keepdims=True)
              for s in range(ATT_WIDTH // LANES)]
    o_ref[0] = jnp.concatenate(pieces, axis=1)


def _moba_decode(q, k_new, v_new, cache_k, cache_v, page_table, layer):
    bd = q.shape[0]
    n_pages = page_table.shape[1]
    page = cache_k.shape[2]
    past_len = n_pages * page
    assert MOBA_BLOCK % page == 0 and past_len % MOBA_BLOCK == 0 and page == LANES
    ppb = MOBA_BLOCK // page
    nb = past_len // MOBA_BLOCK
    n_sel = min(MOBA_TOPK, nb + 1)
    kt = jnp.transpose(cache_k, (0, 1, 3, 4, 2))
    vt = jnp.transpose(cache_v, (0, 1, 3, 4, 2))
    row3 = lambda a: a.reshape(bd, 1, ATT_WIDTH)
    slopes = _alibi_slopes(ATT_HEADS).reshape(ATT_HEADS, 1, 1)
    tok = pl.BlockSpec((1, 1, ATT_WIDTH), lambda bi, pt: (bi, 0, 0))

    def page_spec(pg):
        return pl.BlockSpec((None, None, ATT_HEADS, HEAD_DIM, page),
                            lambda bi, pt: (pt[bi * n_pages + pg], layer, 0, 0, 0))

    kernel = functools.partial(_moba_decode_kernel, pages_per_block=ppb, page=page,
                               past_len=past_len, n_sel=n_sel)
    out = pl.pallas_call(
        kernel,
        out_shape=jax.ShapeDtypeStruct((bd, 1, ATT_WIDTH), F32),
        grid_spec=pltpu.PrefetchScalarGridSpec(
            num_scalar_prefetch=1,
            grid=(bd,),
            in_specs=[tok, tok, tok, pl.BlockSpec((ATT_HEADS, 1, 1), lambda bi, pt: (0, 0, 0))]
                     + [page_spec(pg) for pg in range(n_pages)] * 2,
            out_specs=tok,
        ),
        compiler_params=_params("parallel"),
        name="moba_decode",
    )(page_table.reshape(-1), row3(q), row3(k_new), row3(v_new), slopes,
      *([kt] * n_pages), *([vt] * n_pages))
    return out.reshape(bd, ATT_WIDTH)


def _mix_tail(y_att, y_pool, conv_pre, x, cvec_ref, convpw_ref, wout_ref, ln1_ref,
              rw_ref, rbias_ref, alpha, x1_ref, gates_ref):
    c = _layer_norm(conv_pre + cvec_ref[0:1, :], cvec_ref[1:2, :], cvec_ref[2:3, :])
    y_conv = jnp.dot(_silu(c).astype(BF16), convpw_ref[...], preferred_element_type=F32)
    cat = jnp.concatenate([y_att.astype(BF16), y_pool.astype(BF16), y_conv.astype(BF16)], axis=1)
    mix = jnp.dot(cat, wout_ref[...], preferred_element_type=F32)
    x1 = _layer_norm(alpha * x + mix, ln1_ref[0:1, :], ln1_ref[1:2, :])
    x1_ref[...] = x1

    xh = x1.astype(BF16)
    xl = (x1 - xh.astype(F32)).astype(BF16)
    lhs = jnp.concatenate([xh, xl, xh], axis=1)
    logits_t = jnp.dot(lhs, rw_ref[...], preferred_element_type=F32).T
    e_rows = [logits_t[e:e + 1, :] for e in range(N_EXPERTS)]
    mx = functools.reduce(jnp.maximum, e_rows)
    ex = [jnp.exp(r - mx) for r in e_rows]
    inv_den = 1.0 / functools.reduce(jnp.add, ex)
    probs = [r * inv_den for r in ex]
    sel = [probs[e] + rbias_ref[e:e + 1, :] for e in range(N_EXPERTS)]
    gscore = []
    for g in range(N_EXPERT_GROUPS):
        members = sel[g * EXPERTS_PER_GROUP:(g + 1) * EXPERTS_PER_GROUP]
        pair_sums = [members[i] + members[j] for i in range(EXPERTS_PER_GROUP)
                     for j in range(i + 1, EXPERTS_PER_GROUP)]
        gscore.append(functools.reduce(jnp.maximum, pair_sums))
    best_score = functools.reduce(jnp.maximum, gscore)
    taken = jnp.zeros(best_score.shape, jnp.bool_)
    chosen = []
    for g in range(N_EXPERT_GROUPS):
        is_best = (gscore[g] == best_score) & jnp.logical_not(taken)
        taken = taken | is_best
        members = sel[g * EXPERTS_PER_GROUP:(g + 1) * EXPERTS_PER_GROUP]
        for i in range(EXPERTS_PER_GROUP):
            rank = jnp.zeros(best_score.shape, jnp.int32)
            for j in range(EXPERTS_PER_GROUP):
                if j == i:
                    continue
                ahead = (members[j] >= members[i]) if j < i else (members[j] > members[i])
                rank = rank + ahead.astype(jnp.int32)
            chosen.append(is_best & (rank < EXPERT_TOPK))
    picked = [jnp.where(chosen[e], probs[e], 0.0) for e in range(N_EXPERTS)]
    inv_tot = 1.0 / functools.reduce(jnp.add, picked)
    for e in range(N_EXPERTS):
        gates_ref[e:e + 1, :] = picked[e] * inv_tot


def _pool_lane_select(per_window):
    width = per_window[0].shape[-1]
    gdim = width // POOL_GROUPS
    lane = lax.broadcasted_iota(jnp.int32, per_window[0].shape, 1)
    out = per_window[-1]
    for g in range(POOL_GROUPS - 2, -1, -1):
        out = jnp.where(lane < (g + 1) * gdim, per_window[g], out)
    return out


def _mix_prompt_kernel(yatt_ref, up_ref, uph_ref, glu_ref, gluh_ref, x_ref,
                       poolw_ref, pscale_ref, convw_ref, cvec_ref, convpw_ref, wout_ref, ln1_ref,
                       rw_ref, rbias_ref, x1_ref, gates_ref, pext_ref, cext_ref, cpre_ref,
                       *, tt, tiles_per_seq, alpha):
    i = pl.program_id(0)
    first = (i % tiles_per_seq) == 0
    row0 = (i % tiles_per_seq) * tt
    u = up_ref[...]
    pext_ref[0:POOL_HALO, :] = jnp.where(first, 0.0, uph_ref[...])
    pext_ref[POOL_HALO:, :] = u
    win_sums = []
    cur = u
    for back in range(1, max(POOL_WINDOWS)):
        cur = cur + pext_ref[POOL_HALO - back:POOL_HALO - back + tt, :]
        if back + 1 in POOL_WINDOWS:
            win_sums.append(cur)
    pos1 = (row0 + lax.broadcasted_iota(jnp.int32, (tt, 1), 0) + 1).astype(F32)
    pooled = _pool_lane_select([ws / jnp.minimum(pos1, float(wn)) for ws, wn in zip(win_sums, POOL_WINDOWS)])
    y_pool = jnp.dot((pooled - u).astype(BF16), poolw_ref[...], preferred_element_type=F32) * pscale_ref[...]

    cext_ref[0:CONV_HALO, :] = jnp.where(first, 0.0, gluh_ref[...])
    cext_ref[CONV_HALO:, :] = glu_ref[...]
    lead = CONV_HALO - CONV_HIST
    for c0 in range(0, tt, CONV_ROWS):
        acc = jnp.zeros((CONV_ROWS, glu_ref.shape[1]), F32)
        for j in range(CONV_KERNEL):
            acc = acc + cext_ref[c0 + lead + j:c0 + lead + j + CONV_ROWS, :] * convw_ref[j:j + 1, :]
        cpre_ref[c0:c0 + CONV_ROWS, :] = acc

    _mix_tail(yatt_ref[...], y_pool, cpre_ref[...], x_ref[...], cvec_ref, convpw_ref, wout_ref,
              ln1_ref, rw_ref, rbias_ref, alpha, x1_ref, gates_ref)


def _mix_decode_kernel(yatt_ref, up_ref, pstate_ref, glu_ref, cstate_ref, x_ref,
                       poolw_ref, pscale_ref, convw_ref, cvec_ref, convpw_ref, wout_ref, ln1_ref,
                       rw_ref, rbias_ref, x1_ref, gates_ref, *, past_len, alpha):
    u = up_ref[...]
    ps = pstate_ref[...]
    per_window = []
    for wn in POOL_WINDOWS:
        tot = u + jnp.sum(ps[:, POOL_HIST - (wn - 1):, :], axis=1)
        per_window.append(tot / float(min(past_len + 1, wn)))
    pooled = _pool_lane_select(per_window)
    y_pool = jnp.dot((pooled - u).astype(BF16), poolw_ref[...], preferred_element_type=F32) * pscale_ref[...]
    cw = convw_ref[...]
    conv_pre = (jnp.sum(cstate_ref[...] * cw[None, :CONV_HIST, :], axis=1)
                + glu_ref[...] * cw[CONV_HIST:CONV_KERNEL, :])
    _mix_tail(yatt_ref[...], y_pool, conv_pre, x_ref[...], cvec_ref, convpw_ref, wout_ref,
              ln1_ref, rw_ref, rbias_ref, alpha, x1_ref, gates_ref)


def _mix_weights(lw):
    pool_w = lw["pool_w"]
    g, gd, _ = pool_w.shape
    blockdiag = jnp.zeros((g * gd, g * gd), F32)
    for i in range(g):
        blockdiag = blockdiag.at[i * gd:(i + 1) * gd, i * gd:(i + 1) * gd].set(pool_w[i])
    rw = lw["router_w"]
    rw_h = rw.astype(BF16)
    rw_l = (rw - rw_h.astype(F32)).astype(BF16)
    rw3 = jnp.pad(jnp.concatenate([rw_h, rw_h, rw_l], axis=0), ((0, 0), (0, LANES - N_EXPERTS)))
    return dict(
        poolw=blockdiag.astype(BF16),
        pscale=lw["pool_scale"].reshape(1, -1),
        convw=lw["conv_w"],
        cvec=jnp.stack([lw["conv_b"], lw["conv_ln_g"], lw["conv_ln_b"]]),
        convpw=lw["conv_pw"].astype(BF16),
        wout=lw["w_out"].astype(BF16),
        ln1=jnp.stack([lw["ln1_g"], lw["ln1_b"]]),
        rw=rw3,
        rbias=lw["router_bias"].reshape(N_EXPERTS, 1),
    )


_MIX_W_ORDER = ("poolw", "pscale", "convw", "cvec", "convpw", "wout", "ln1", "rw", "rbias")


def _full_spec(a):
    nd = a.ndim
    return pl.BlockSpec(a.shape, lambda *_: (0,) * nd)


def _mix_prompt(y_att, u_pool, glu, x, mw, *, t, alpha):
    m, d = x.shape
    tt = min(TT_MIX, t)
    assert t % tt == 0 and tt % CONV_ROWS == 0 and tt % CONV_HALO == 0
    tiles_per_seq = t // tt
    pw, cw = u_pool.shape[1], glu.shape[1]
    row = lambda width: pl.BlockSpec((tt, width), lambda i: (i, 0))
    halo = lambda rows, width: pl.BlockSpec(
        (rows, width), lambda i: (jnp.maximum(i * (tt // rows) - 1, 0), 0))
    weights = [mw[k] for k in _MIX_W_ORDER]
    kernel = functools.partial(_mix_prompt_kernel, tt=tt, tiles_per_seq=tiles_per_seq, alpha=alpha)
    return pl.pallas_call(
        kernel,
        out_shape=[jax.ShapeDtypeStruct((m, d), F32), jax.ShapeDtypeStruct((N_EXPERTS, m), F32)],
        grid=(m // tt,),
        in_specs=[row(y_att.shape[1]), row(pw), halo(POOL_HALO, pw), row(cw), halo(CONV_HALO, cw), row(d)]
                 + [_full_spec(w) for w in weights],
        out_specs=[row(d), pl.BlockSpec((N_EXPERTS, tt), lambda i: (0, i))],
        scratch_shapes=[pltpu.VMEM((POOL_HALO + tt, pw), F32), pltpu.VMEM((CONV_HALO + tt, cw), F32),
                        pltpu.VMEM((tt, cw), F32)],
        compiler_params=_params("parallel"),
        name="mix_prompt",
    )(y_att, u_pool, u_pool, glu, glu, x, *weights)


def _mix_decode(y_att, u_pool, pool_state, glu, conv_state, x, mw, *, past_len, alpha):
    m, d = x.shape
    weights = [mw[k] for k in _MIX_W_ORDER]
    ins = [y_att, u_pool, pool_state, glu, conv_state, x] + weights
    kernel = functools.partial(_mix_decode_kernel, past_len=past_len, alpha=alpha)
    return pl.pallas_call(
        kernel,
        out_shape=[jax.ShapeDtypeStruct((m, d), F32), jax.ShapeDtypeStruct((N_EXPERTS, m), F32)],
        grid=(1,),
        in_specs=[_full_spec(a) for a in ins],
        out_specs=[pl.BlockSpec((m, d), lambda i: (0, 0)), pl.BlockSpec((N_EXPERTS, m), lambda i: (0, 0))],
        compiler_params=_params("arbitrary"),
        name="mix_decode",
    )(*ins)


def _moe_kernel(x_ref, gates_ref, wg_ref, wu_ref, wd_ref, ln_ref, o_ref, xb_ref, acc_ref, *, alpha):
    e = pl.program_id(1)

    @pl.when(e == 0)
    def _():
        xb_ref[...] = x_ref[...].astype(BF16)
        acc_ref[...] = jnp.zeros(acc_ref.shape, F32)

    xb = xb_ref[...]
    hg = jnp.dot(xb, wg_ref[0].astype(BF16), preferred_element_type=F32)
    hu = jnp.dot(xb, wu_ref[0].astype(BF16), preferred_element_type=F32)
    gates = gates_ref[...]
    lane = lax.broadcasted_iota(jnp.int32, gates.shape, 1)
    gate_e = jnp.sum(jnp.where(lane == e, gates, 0.0), axis=1, keepdims=True)
    hidden = (_silu(hg) * hu * gate_e).astype(BF16)
    acc_ref[...] += jnp.dot(hidden, wd_ref[0].astype(BF16), preferred_element_type=F32)

    @pl.when(e == pl.num_programs(1) - 1)
    def _():
        o_ref[...] = _layer_norm(alpha * x_ref[...] + acc_ref[...], ln_ref[0:1, :], ln_ref[1:2, :])


def _moe(x, gates, exp_gate, exp_up, exp_down, ln2, *, layer, alpha):
    m, d = x.shape
    tm = min(TM_MOE, m)
    assert m % tm == 0
    de = exp_gate.shape[-1]
    return pl.pallas_call(
        functools.partial(_moe_kernel, alpha=alpha),
        out_shape=jax.ShapeDtypeStruct((m, d), F32),
        grid=(m // tm, N_EXPERTS),
        in_specs=[
            pl.BlockSpec((tm, d), lambda i, e: (i, 0)),
            pl.BlockSpec((tm, N_EXPERTS), lambda i, e: (i, 0)),
            pl.BlockSpec((None, 1, d, de), lambda i, e: (layer, e, 0, 0)),
            pl.BlockSpec((None, 1, d, de), lambda i, e: (layer, e, 0, 0)),
            pl.BlockSpec((None, 1, de, d), lambda i, e: (layer, e, 0, 0)),
            pl.BlockSpec((2, d), lambda i, e: (0, 0)),
        ],
        out_specs=pl.BlockSpec((tm, d), lambda i, e: (i, 0)),
        scratch_shapes=[pltpu.VMEM((tm, d), BF16), pltpu.VMEM((tm, d), F32)],
        compiler_params=_params("parallel", "arbitrary"),
        name="moe",
    )(x, gates, exp_gate, exp_up, exp_down, ln2)


def _moe_plan(gates_t, tm, chunk):
    m = gates_t.shape[1]
    tiles = m // tm
    gsum = gates_t.reshape(N_EXPERT_GROUPS, EXPERTS_PER_GROUP, m).sum(axis=1)
    gid = jnp.argmax(gsum, axis=0).astype(jnp.int32).reshape(tiles, tm)
    onehot = (gid[:, None, :] == jnp.arange(N_EXPERT_GROUPS, dtype=jnp.int32)[None, :, None]).astype(jnp.int32)
    csum = jnp.cumsum(onehot, axis=2)
    nch = (csum[:, :, -1] + chunk - 1) // chunk
    start = (jnp.cumsum(nch, axis=1) - nch) * chunk
    dst = jnp.sum(onehot * (start[:, :, None] + csum - 1), axis=1)
    return dst.reshape(-1), start.reshape(-1), nch.reshape(-1)


def _moe_grouped_kernel(dst_ref, start_ref, nch_ref, x_ref, gates_ref, wg_ref, wu_ref, wd_ref, ln_ref,
                        o_ref, xs_ref, gs_ref, acc_ref, *, alpha, tm, chunk):
    i = pl.program_id(0)
    e = pl.program_id(1)
    half = x_ref.shape[1] // 2

    @pl.when(e == 0)
    def _():
        def pack(c, carry):
            r0 = pl.multiple_of(c * chunk, chunk)
            lo = pltpu.bitcast(x_ref[pl.ds(r0, chunk), :half].astype(BF16).astype(F32), jnp.uint32)
            hi = pltpu.bitcast(x_ref[pl.ds(r0, chunk), half:].astype(BF16).astype(F32), jnp.uint32)
            o_ref[pl.ds(r0, chunk), :half] = pltpu.bitcast((hi & jnp.uint32(0xFFFF0000)) | (lo >> 16), F32)
            return carry

        lax.fori_loop(0, tm // chunk, pack, 0)
        xs_ref[...] = jnp.zeros(xs_ref.shape, F32)
        gs_ref[...] = jnp.zeros(gs_ref.shape, F32)
        acc_ref[...] = jnp.zeros(acc_ref.shape, F32)

        def put(r, carry):
            d = dst_ref[i * tm + r]
            xs_ref[pl.ds(d, 1), :] = o_ref[pl.ds(r, 1), :half]
            gs_ref[pl.ds(d, 1), :] = gates_ref[pl.ds(r, 1), :]
            return carry

        lax.fori_loop(0, tm, put, 0, unroll=8)

    g = e // EXPERTS_PER_GROUP
    start = start_ref[i * N_EXPERT_GROUPS + g]
    wg = wg_ref[0].astype(BF16)
    wu = wu_ref[0].astype(BF16)
    wd = wd_ref[0].astype(BF16)
    lane = lax.broadcasted_iota(jnp.int32, (chunk, gs_ref.shape[1]), 1)

    def rows_of(c, carry):
        r0 = pl.multiple_of(start + c * chunk, chunk)
        words = pltpu.bitcast(xs_ref[pl.ds(r0, chunk), :], jnp.uint32)
        x_lo = pltpu.bitcast(words << 16, F32).astype(BF16)
        x_hi = pltpu.bitcast(words & jnp.uint32(0xFFFF0000), F32).astype(BF16)
        hg = (jnp.dot(x_lo, wg[:half], preferred_element_type=F32)
              + jnp.dot(x_hi, wg[half:], preferred_element_type=F32))
        hu = (jnp.dot(x_lo, wu[:half], preferred_element_type=F32)
              + jnp.dot(x_hi, wu[half:], preferred_element_type=F32))
        gate_e = jnp.sum(jnp.where(lane == e, gs_ref[pl.ds(r0, chunk), :], 0.0), axis=1, keepdims=True)
        hidden = (_silu(hg) * hu * gate_e).astype(BF16)
        acc_ref[pl.ds(r0, chunk), :] += jnp.dot(hidden, wd, preferred_element_type=F32)
        return carry

    lax.fori_loop(0, nch_ref[i * N_EXPERT_GROUPS + g], rows_of, 0)

    @pl.when(e == pl.num_programs(1) - 1)
    def _():
        def take(r, carry):
            d = dst_ref[i * tm + r]
            o_ref[pl.ds(r, 1), :] = acc_ref[pl.ds(d, 1), :]
            return carry

        lax.fori_loop(0, tm, take, 0, unroll=8)

        def norm(c, carry):
            rows = pl.ds(pl.multiple_of(c * chunk, chunk), chunk)
            o_ref[rows, :] = _layer_norm(alpha * x_ref[rows, :] + o_ref[rows, :], ln_ref[0:1, :], ln_ref[1:2, :])
            return carry

        lax.fori_loop(0, tm // chunk, norm, 0)


def _moe_grouped(x, gates_t, exp_gate, exp_up, exp_down, ln2, *, layer, alpha):
    m, d = x.shape
    tm, chunk = TM_MOE_GROUPED, MOE_CHUNK
    assert m % tm == 0
    de = exp_gate.shape[-1]
    sort_rows = tm + N_EXPERT_GROUPS * chunk
    dst, start, nch = _moe_plan(gates_t, tm, chunk)
    once = pl.Buffered(1)
    return pl.pallas_call(
        functools.partial(_moe_grouped_kernel, alpha=alpha, tm=tm, chunk=chunk),
        out_shape=jax.ShapeDtypeStruct((m, d), F32),
        grid_spec=pltpu.PrefetchScalarGridSpec(
            num_scalar_prefetch=3,
            grid=(m // tm, N_EXPERTS),
            in_specs=[
                pl.BlockSpec((tm, d), lambda i, e, *_: (i, 0), pipeline_mode=once),
                pl.BlockSpec((tm, N_EXPERTS), lambda i, e, *_: (i, 0), pipeline_mode=once),
                pl.BlockSpec((None, 1, d, de), lambda i, e, *_: (layer, e, 0, 0)),
                pl.BlockSpec((None, 1, d, de), lambda i, e, *_: (layer, e, 0, 0)),
                pl.BlockSpec((None, 1, de, d), lambda i, e, *_: (layer, e, 0, 0)),
                pl.BlockSpec((2, d), lambda i, e, *_: (0, 0)),
            ],
            out_specs=pl.BlockSpec((tm, d), lambda i, e, *_: (i, 0), pipeline_mode=once),
            scratch_shapes=[pltpu.VMEM((sort_rows, d // 2), F32), pltpu.VMEM((sort_rows, N_EXPERTS), F32),
                            pltpu.VMEM((sort_rows, d), F32)],
        ),
        compiler_params=pltpu.CompilerParams(dimension_semantics=("parallel", "arbitrary"),
                                             vmem_limit_bytes=VMEM_LIMIT_MOE),
        name="moe_grouped",
    )(dst, start, nch, x, gates_t.T, exp_gate, exp_up, exp_down, ln2)


def kernel(x_prompt, x_sample, cache_k, cache_v, state_pool, state_conv, page_table, w_in, pool_w, pool_scale, conv_w, conv_b, conv_ln_g, conv_ln_b, conv_pw, w_out, ln1_g, ln1_b, router_w, router_bias, exp_gate, exp_up, exp_down, ln2_g, ln2_b):
    b, t, d = x_prompt.shape
    bd, td, _ = x_sample.shape
    assert td == 1
    depth = w_in.shape[0]
    alpha = (2 * depth) ** 0.25
    past_len = page_table.shape[1] * cache_k.shape[2]

    hp = x_prompt.reshape(b * t, d)
    hs = x_sample.reshape(bd, d)
    kp_l, vp_l, ks_l, vs_l, pp_l, ps_l, cp_l, cs_l = [], [], [], [], [], [], [], []
    for l in range(depth):
        lw = dict(pool_w=pool_w[l], pool_scale=pool_scale[l], conv_w=conv_w[l], conv_b=conv_b[l],
                  conv_ln_g=conv_ln_g[l], conv_ln_b=conv_ln_b[l], conv_pw=conv_pw[l], w_out=w_out[l],
                  ln1_g=ln1_g[l], ln1_b=ln1_b[l], router_w=router_w, router_bias=router_bias)
        mw = _mix_weights(lw)
        w_in_bf = w_in[l].astype(BF16)
        wqkvt_bf = w_in_bf[:, :3 * ATT_WIDTH].T
        ln2 = jnp.stack([ln2_g[l], ln2_b[l]])

        qt, kb, kmean, vtb, u_pool, glu, kt, vt = _in_proj_prompt(hp, w_in_bf, wqkvt_bf, b=b, t=t)
        y_att = _moba_prompt(qt, kb, vtb, kmean, b, t)
        x1, gates_t = _mix_prompt(y_att, u_pool, glu, hp, mw, t=t, alpha=alpha)
        hp = _moe_grouped(x1, gates_t, exp_gate, exp_up, exp_down, ln2, layer=l, alpha=alpha)
        kp_l.append(kt)
        vp_l.append(vt)
        pp_l.append(u_pool.reshape(b, t, -1)[:, t - POOL_HIST:])
        cp_l.append(glu.reshape(b, t, -1)[:, t - CONV_HIST:])

        qs, ks, vs, us, gs = _in_proj_decode(hs, w_in_bf)
        ys = _moba_decode(qs, ks, vs, cache_k, cache_v, page_table, l)
        x1s, gates_ts = _mix_decode(ys, us, state_pool[:, l], gs, state_conv[:, l], hs, mw,
                                    past_len=past_len, alpha=alpha)
        hs = _moe(x1s, gates_ts.T, exp_gate, exp_up, exp_down, ln2, layer=l, alpha=alpha)
        ks_l.append(ks.reshape(bd, 1, ATT_HEADS, HEAD_DIM))
        vs_l.append(vs.reshape(bd, 1, ATT_HEADS, HEAD_DIM))
        ps_l.append(jnp.concatenate([state_pool[:, l, 1:], us[:, None, :]], axis=1))
        cs_l.append(jnp.concatenate([state_conv[:, l, 1:], gs[:, None, :]], axis=1))

    stack = lambda xs: jnp.stack(xs, axis=1)

    def heads_last(xs):
        return jnp.transpose(stack(xs).reshape(b, depth, ATT_HEADS, HEAD_DIM, t), (0, 1, 4, 2, 3))

    return (hp.reshape(b, t, d), hs.reshape(bd, 1, d), heads_last(kp_l), heads_last(vp_l), stack(ks_l), stack(vs_l),
            stack(pp_l), stack(ps_l), stack(cp_l), stack(cs_l))
```

```python
import functools
import math

import jax
import jax.numpy as jnp
from jax import lax
from jax.experimental import pallas as pl
from jax.experimental.pallas import tpu as pltpu

F32 = jnp.float32
BF16 = jnp.bfloat16

ATT_HEADS = 8
HEAD_DIM = 64
ATT_WIDTH = ATT_HEADS * HEAD_DIM
POOL_WINDOWS = (2, 4, 8, 16)
POOL_GROUPS = len(POOL_WINDOWS)
POOL_HIST = max(POOL_WINDOWS) - 1
CONV_KERNEL = 31
CONV_HIST = CONV_KERNEL - 1
MOBA_BLOCK = 256
MOBA_TOPK = 3
N_EXPERTS = 16
N_EXPERT_GROUPS = 4
EXPERTS_PER_GROUP = N_EXPERTS // N_EXPERT_GROUPS
EXPERT_TOPK = 2
LN_EPS = 1e-5
NEG_BIG = -1e30

LOG2_E = math.log2(math.e)
LANES = 128
SUBLANES = 8
HEADS_PER_TILE = LANES // HEAD_DIM
BF16_ROWS = 2 * SUBLANES
V_ROWS = HEAD_DIM + BF16_ROWS
POOL_HALO = 16
CONV_HALO = 32
VMEM_LIMIT = 56 * 1024 * 1024

TM_IN = 512
TT_MIX = 512
TM_MOE = 1024
TM_MOE_GROUPED = 2048
MOE_CHUNK = 256
VMEM_LIMIT_MOE = 60 * 1024 * 1024
CONV_ROWS = 64


def _params(*sem):
    return pltpu.CompilerParams(dimension_semantics=sem, vmem_limit_bytes=VMEM_LIMIT)


def _sigmoid(x):
    return 1.0 / (1.0 + jnp.exp(-x))


def _silu(x):
    return x * _sigmoid(x)


def _layer_norm(x, g, b):
    mu = jnp.mean(x, axis=-1, keepdims=True)
    xc = x - mu
    var = jnp.mean(xc * xc, axis=-1, keepdims=True)
    return xc * lax.rsqrt(var + LN_EPS) * g + b


def _alibi_slopes(n_heads):
    return 2.0 ** (-8.0 * jnp.arange(1, n_heads + 1, dtype=F32) / n_heads)


def _proj_segments(xb, w_ref, pw, cw):
    aw = ATT_WIDTH

    def seg(lo, width):
        return jnp.dot(xb, w_ref[:, lo:lo + width], preferred_element_type=F32)

    glu = seg(3 * aw + pw, cw) * _sigmoid(seg(3 * aw + pw + cw, cw))
    return seg, seg(3 * aw, pw), glu


def _in_proj_decode_kernel(x_ref, w_ref, q_ref, k_ref, v_ref, up_ref, glu_ref):
    xb = x_ref[...].astype(BF16)
    seg, up, glu = _proj_segments(xb, w_ref, up_ref.shape[-1], glu_ref.shape[-1])
    q_ref[...] = seg(0, ATT_WIDTH)
    k_ref[...] = seg(ATT_WIDTH, ATT_WIDTH)
    v_ref[...] = seg(2 * ATT_WIDTH, ATT_WIDTH)
    up_ref[...] = up
    glu_ref[...] = glu


def _in_proj_prompt_kernel(x_ref, w_ref, wqkvt_ref, *refs, tm):
    qt_ref, kb_ref, km_ref, vtb_ref, up_ref, glu_ref, kt_ref, vt_ref = refs[-8:]
    aw = ATT_WIDTH
    blk = MOBA_BLOCK
    xb = x_ref[...].astype(BF16)
    seg, up, glu = _proj_segments(xb, w_ref, up_ref.shape[-1], glu_ref.shape[-1])
    up_ref[...] = up
    glu_ref[...] = glu
    k = seg(aw, aw)
    kb_ref[...] = k.astype(BF16)
    for i in range(tm // blk):
        km_ref[i] = jnp.mean(k[i * blk:(i + 1) * blk], axis=0, keepdims=True)
    qkvt = lax.dot_general(wqkvt_ref[...], xb, (((1,), (1,)), ((), ())), preferred_element_type=F32)
    qt_ref[...] = qkvt[:aw]
    kt_ref[...] = qkvt[aw:2 * aw]
    vt = qkvt[2 * aw:]
    vt_ref[...] = vt
    tail_row = lax.broadcasted_iota(jnp.int32, (V_ROWS - HEAD_DIM, blk), 0)
    tail = jnp.where(tail_row == 0, 1.0, 0.0).astype(BF16)
    for i in range(tm // blk):
        vb = vt[:, i * blk:(i + 1) * blk].astype(BF16)
        vtb_ref[i] = jnp.concatenate(
            [piece for h in range(ATT_HEADS) for piece in (vb[h * HEAD_DIM:(h + 1) * HEAD_DIM], tail)], axis=0)


def _split_widths(n):
    pw = (n - 3 * ATT_WIDTH) // 3
    return pw, pw


def _in_proj_decode(x, w_bf):
    m, d = x.shape
    n = w_bf.shape[1]
    pw, cw = _split_widths(n)
    full = lambda width: pl.BlockSpec((m, width), lambda i: (0, 0))
    return pl.pallas_call(
        _in_proj_decode_kernel,
        out_shape=[jax.ShapeDtypeStruct((m, ATT_WIDTH), F32)] * 3 + [
            jax.ShapeDtypeStruct((m, pw), F32), jax.ShapeDtypeStruct((m, cw), F32)],
        grid=(1,),
        in_specs=[full(d), pl.BlockSpec((d, n), lambda i: (0, 0))],
        out_specs=[full(ATT_WIDTH)] * 3 + [full(pw), full(cw)],
        compiler_params=_params("arbitrary"),
        name="in_proj_decode",
    )(x, w_bf)


def _in_proj_prompt(x, w_bf, wqkvt_bf, kv_all, *, b, t, layer, depth):
    m, d = x.shape
    n = w_bf.shape[1]
    aw = ATT_WIDTH
    blk = MOBA_BLOCK
    pw, cw = _split_widths(n)
    tm = min(TM_IN, t)
    assert t % tm == 0 and tm % blk == 0
    tps = t // tm
    nblk = tm // blk
    row = lambda width: pl.BlockSpec((tm, width), lambda i: (i, 0))
    tok_lanes = pl.BlockSpec((None, aw, tm), lambda i: (i // tps, 0, i % tps))
    layer_slab = pl.BlockSpec((None, None, aw, tm), lambda i: (i // tps, layer, 0, i % tps))
    all_layers = jax.ShapeDtypeStruct((b, depth, aw, t), F32)
    carried = [] if kv_all is None else list(kv_all)
    n_fixed = 3
    return pl.pallas_call(
        functools.partial(_in_proj_prompt_kernel, tm=tm),
        out_shape=[jax.ShapeDtypeStruct((b, aw, t), F32), jax.ShapeDtypeStruct((m, aw), BF16),
                   jax.ShapeDtypeStruct((m // blk, 1, aw), F32),
                   jax.ShapeDtypeStruct((m // blk, ATT_HEADS * V_ROWS, blk), BF16),
                   jax.ShapeDtypeStruct((m, pw), F32), jax.ShapeDtypeStruct((m, cw), F32),
                   all_layers, all_layers],
        grid=(m // tm,),
        in_specs=[row(d), pl.BlockSpec((d, n), lambda i: (0, 0)), pl.BlockSpec((3 * aw, d), lambda i: (0, 0))]
                 + [pl.BlockSpec(memory_space=pl.ANY)] * len(carried),
        out_specs=[tok_lanes, row(aw), pl.BlockSpec((nblk, 1, aw), lambda i: (i, 0, 0)),
                   pl.BlockSpec((nblk, ATT_HEADS * V_ROWS, blk), lambda i: (i, 0, 0)), row(pw), row(cw),
                   layer_slab, layer_slab],
        input_output_aliases={n_fixed + k: 6 + k for k in range(len(carried))},
        compiler_params=_params("parallel"),
        name="in_proj_prompt",
    )(x, w_bf, wqkvt_bf, *carried)


def _moba_prompt_kernel(slopes_ref, q_ref, k_ref, vt_ref, km_ref, ab_ref,
                        o_ref, qt_ref, sel_ref, sa_ref, sb_ref, pa_ref, pb_ref, *, n_sel):
    blk = MOBA_BLOCK
    hp = pl.program_id(1)
    qi = pl.program_id(2)
    q_t = q_ref[...]
    km = km_ref[0]
    nbp = km.shape[0]
    scale = HEAD_DIM ** -0.5 * LOG2_E

    km_lane = lax.broadcasted_iota(jnp.int32, km.shape, 1)
    km_heads = jnp.concatenate(
        [jnp.where((km_lane >= hh * HEAD_DIM) & (km_lane < (hh + 1) * HEAD_DIM), km, 0.0)
         for hh in range(HEADS_PER_TILE)], axis=0)
    gates = jnp.dot(km_heads, q_t, precision=lax.Precision.HIGHEST, preferred_element_type=F32)
    row = lax.broadcasted_iota(jnp.int32, (nbp, blk), 0)
    row_f = row.astype(F32)
    dim = lax.broadcasted_iota(jnp.int32, (LANES, blk), 0)
    for hh in range(HEADS_PER_TILE):
        gate = jnp.where(row < qi, gates[hh * nbp:(hh + 1) * nbp], -jnp.inf)
        sel = jnp.zeros((nbp, blk), jnp.bool_)
        for _ in range(n_sel):
            mx = jnp.max(gate, axis=0, keepdims=True)
            first = jnp.min(jnp.where(gate == mx, row_f, float(nbp)), axis=0, keepdims=True)
            pick = (row_f == first) & (mx > -jnp.inf)
            sel = sel | pick
            gate = jnp.where(pick, -jnp.inf, gate)
        sel_ref[hh] = jnp.where(sel | (row == qi), 1.0, 0.0)
        in_head = (dim >= hh * HEAD_DIM) & (dim < (hh + 1) * HEAD_DIM)
        qt_ref[hh] = (jnp.where(in_head, q_t, 0.0) * scale).astype(BF16)

    slopes = [slopes_ref[hp * HEADS_PER_TILE + hh] for hh in range(HEADS_PER_TILE)]

    def scores(j, slot_ref):
        jc = jnp.minimum(j, qi)
        start = pl.multiple_of(jc * blk, blk)
        kj = k_ref[0, pl.ds(start, blk), :]
        own = (jc == qi).astype(jnp.int32)
        for hh in range(HEADS_PER_TILE):
            slot_ref[hh] = jnp.dot(kj, qt_ref[hh], preferred_element_type=F32) + ab_ref[own, hh]

    def softmax(j, slot_ref, p_ref, ms):
        jc = jnp.minimum(j, qi)
        dist = ((qi - jc) * blk).astype(F32)
        new_ms, alphas = [], []
        for hh in range(HEADS_PER_TILE):
            s = slot_ref[hh]
            picked = (sel_ref[hh, pl.ds(jc, 1), :] > 0.5) & (j <= qi)
            c = -(slopes[hh] * dist)
            m_new = jnp.maximum(ms[hh], jnp.max(s, axis=0, keepdims=True) + c)
            shift = jnp.where(picked, m_new - c, -NEG_BIG)
            m_new = jnp.where(picked, m_new, ms[hh])
            p_ref[hh] = jnp.exp2(s - shift).astype(BF16)
            alphas.append(jnp.exp2(ms[hh] - m_new))
            new_ms.append(m_new)
        return tuple(new_ms), tuple(alphas)

    def weighted_values(j, p_ref, alphas, accs):
        vtj = vt_ref[jnp.clip(j, 0, qi)]
        return tuple(accs[hh] * alphas[hh]
                     + jnp.dot(vtj[hh * V_ROWS:(hh + 1) * V_ROWS, :], p_ref[hh], preferred_element_type=F32)
                     for hh in range(HEADS_PER_TILE))

    def pair(jj, carry):
        ms, accs, pending = carry
        j0 = 2 * jj
        scores(j0 + 1, sb_ref)
        accs = weighted_values(j0 - 1, pb_ref, pending, accs)
        ms, alphas = softmax(j0, sa_ref, pa_ref, ms)
        scores(j0 + 2, sa_ref)
        accs = weighted_values(j0, pa_ref, alphas, accs)
        ms, pending = softmax(j0 + 1, sb_ref, pb_ref, ms)
        return ms, accs, pending

    heads = range(HEADS_PER_TILE)
    init = (tuple(jnp.full((1, blk), NEG_BIG, F32) for _ in heads),
            tuple(jnp.zeros((V_ROWS, blk), F32) for _ in heads),
            tuple(jnp.ones((1, blk), F32) for _ in heads))
    pb_ref[...] = jnp.zeros(pb_ref.shape, BF16)
    scores(jnp.int32(0), sa_ref)
    n_pairs = qi // 2 + 1
    ms, accs, pending = lax.fori_loop(0, n_pairs, pair, init)
    accs = weighted_values(2 * n_pairs - 1, pb_ref, pending, accs)
    out_t = jnp.concatenate([accs[hh][:HEAD_DIM] * (1.0 / accs[hh][HEAD_DIM:HEAD_DIM + 1])
                             for hh in range(HEADS_PER_TILE)], axis=0)
    o_ref[...] = out_t.T.astype(o_ref.dtype)


def _moba_prompt(qt, kb, vtb, kmean, b, t):
    blk = MOBA_BLOCK
    nb = t // blk
    assert t % blk == 0
    nbp = -(-nb // SUBLANES) * SUBLANES
    n_sel = min(MOBA_TOPK, nb)
    slopes = _alibi_slopes(ATT_HEADS) * LOG2_E
    r = jnp.arange(blk, dtype=F32)
    rel = r[None, :] - r[:, None]
    ab_off = -slopes[:, None, None] * rel[None]
    ab = jnp.stack([ab_off, jnp.where(rel[None] >= 0, ab_off, NEG_BIG)])
    km = jnp.pad(kmean.reshape(b, nb, ATT_WIDTH), ((0, 0), (0, nbp - nb), (0, 0)))
    kb3 = kb.reshape(b, t, ATT_WIDTH)
    n_hp = ATT_HEADS // HEADS_PER_TILE
    return pl.pallas_call(
        functools.partial(_moba_prompt_kernel, n_sel=n_sel),
        out_shape=jax.ShapeDtypeStruct((b * t, ATT_WIDTH), BF16),
        grid=(b, n_hp, nb),
        in_specs=[
            pl.BlockSpec(memory_space=pltpu.SMEM),
            pl.BlockSpec((None, LANES, blk), lambda bi, hp, qi: (bi, hp, qi)),
            pl.BlockSpec((1, t, LANES), lambda bi, hp, qi: (bi, 0, hp)),
            pl.BlockSpec((nb, HEADS_PER_TILE * V_ROWS, blk), lambda bi, hp, qi: (bi, hp, 0)),
            pl.BlockSpec((1, nbp, LANES), lambda bi, hp, qi: (bi, 0, hp)),
            pl.BlockSpec((2, HEADS_PER_TILE, blk, blk), lambda bi, hp, qi: (0, hp, 0, 0)),
        ],
        out_specs=pl.BlockSpec((blk, LANES), lambda bi, hp, qi: (bi * nb + qi, hp)),
        scratch_shapes=[pltpu.VMEM((HEADS_PER_TILE, LANES, blk), BF16),
                        pltpu.VMEM((HEADS_PER_TILE, nbp, blk), F32),
                        pltpu.VMEM((HEADS_PER_TILE, blk, blk), F32),
                        pltpu.VMEM((HEADS_PER_TILE, blk, blk), F32),
                        pltpu.VMEM((HEADS_PER_TILE, blk, blk), BF16),
                        pltpu.VMEM((HEADS_PER_TILE, blk, blk), BF16)],
        compiler_params=_params("parallel", "parallel", "arbitrary"),
        name="moba_prompt",
    )(slopes, qt, kb3, vtb, km, ab)


def _lane_bcast_cols(row):
    slabs = [jnp.broadcast_to(row[:, s * LANES:(s + 1) * LANES], (LANES, LANES)).T
             for s in range(ATT_WIDTH // LANES)]
    return jnp.concatenate(slabs, axis=0).reshape(ATT_HEADS, HEAD_DIM, LANES)


def _moba_decode_kernel(pt_ref, q_ref, kn_ref, vn_ref, slopes_ref, *rest,
                        pages_per_block, page, past_len, n_sel):
    n_pages = (len(rest) - 1) // 2
    kt_refs = rest[:n_pages]
    vt_refs = rest[n_pages:2 * n_pages]
    o_ref = rest[2 * n_pages]
    nb = n_pages // pages_per_block
    scale = HEAD_DIM ** -0.5
    qb = _lane_bcast_cols(q_ref[0])
    slopes = slopes_ref[...]
    lane_f = lax.broadcasted_iota(jnp.int32, (1, 1, page), 2).astype(F32)

    scores, gates, blk_max = [], [], []
    for i in range(nb):
        raw_sum, top = None, None
        for pg in range(i * pages_per_block, (i + 1) * pages_per_block):
            raw = jnp.sum(kt_refs[pg][...] * qb, axis=1, keepdims=True)
            s = raw * scale - slopes * (float(past_len - pg * page) - lane_f)
            scores.append(s)
            raw_sum = raw if raw_sum is None else raw_sum + raw
            top = s if top is None else jnp.maximum(top, s)
        gates.append(jnp.sum(raw_sum, axis=2, keepdims=True) * (1.0 / MOBA_BLOCK))
        blk_max.append(jnp.max(top, axis=2, keepdims=True))

    knb = _lane_bcast_cols(kn_ref[0])
    vnb = _lane_bcast_cols(vn_ref[0])
    s_own = jnp.max(jnp.sum(knb * qb, axis=1, keepdims=True), axis=2, keepdims=True) * scale
    m_tot = s_own
    sels = []
    for i in range(nb):
        rank = jnp.zeros(gates[i].shape, jnp.int32)
        for j in range(nb):
            if j == i:
                continue
            ahead = (gates[j] >= gates[i]) if j < i else (gates[j] > gates[i])
            rank = rank + ahead.astype(jnp.int32)
        sel = rank < n_sel
        sels.append(sel)
        m_tot = jnp.where(sel, jnp.maximum(m_tot, blk_max[i]), m_tot)

    w_own = jnp.exp(s_own - m_tot)
    lane = lax.broadcasted_iota(jnp.int32, vnb.shape, 2)
    num = jnp.where(lane == 0, w_own * vnb, 0.0)
    den_lanes = jnp.zeros(scores[0].shape, F32)
    for pg in range(n_pages):
        w = jnp.where(sels[pg // pages_per_block], jnp.exp(scores[pg] - m_tot), 0.0)
        den_lanes = den_lanes + w
        num = num + w * vt_refs[pg][...]
    den = w_own + jnp.sum(den_lanes, axis=2, keepdims=True)
    flat = (num * (1.0 / den)).reshape(ATT_WIDTH, LANES)
    pieces = [jnp.sum(flat[s * LANES:(s + 1) * LANES, :].T, axis=0, keepdims=True)
              for s in range(ATT_WIDTH // LANES)]
    o_ref[0] = jnp.concatenate(pieces, axis=1)


def _moba_decode(q, k_new, v_new, cache_k, cache_v, page_table, layer):
    bd = q.shape[0]
    n_pages = page_table.shape[1]
    page = cache_k.shape[2]
    past_len = n_pages * page
    assert MOBA_BLOCK % page == 0 and past_len % MOBA_BLOCK == 0 and page == LANES
    ppb = MOBA_BLOCK // page
    nb = past_len // MOBA_BLOCK
    n_sel = min(MOBA_TOPK, nb + 1)
    kt = jnp.transpose(cache_k, (0, 1, 3, 4, 2))
    vt = jnp.transpose(cache_v, (0, 1, 3, 4, 2))
    row3 = lambda a: a.reshape(bd, 1, ATT_WIDTH)
    slopes = _alibi_slopes(ATT_HEADS).reshape(ATT_HEADS, 1, 1)
    tok = pl.BlockSpec((1, 1, ATT_WIDTH), lambda bi, pt: (bi, 0, 0))

    def page_spec(pg):
        return pl.BlockSpec((None, None, ATT_HEADS, HEAD_DIM, page),
                            lambda bi, pt: (pt[bi * n_pages + pg], layer, 0, 0, 0))

    kernel = functools.partial(_moba_decode_kernel, pages_per_block=ppb, page=page,
                               past_len=past_len, n_sel=n_sel)
    out = pl.pallas_call(
        kernel,
        out_shape=jax.ShapeDtypeStruct((bd, 1, ATT_WIDTH), F32),
        grid_spec=pltpu.PrefetchScalarGridSpec(
            num_scalar_prefetch=1,
            grid=(bd,),
            in_specs=[tok, tok, tok, pl.BlockSpec((ATT_HEADS, 1, 1), lambda bi, pt: (0, 0, 0))]
                     + [page_spec(pg) for pg in range(n_pages)] * 2,
            out_specs=tok,
        ),
        compiler_params=_params("parallel"),
        name="moba_decode",
    )(page_table.reshape(-1), row3(q), row3(k_new), row3(v_new), slopes,
      *([kt] * n_pages), *([vt] * n_pages))
    return out.reshape(bd, ATT_WIDTH)


def _mix_tail(y_att, y_pool, conv_pre, x, cvec_ref, convpw_ref, wout_ref, ln1_ref,
              rw_ref, rbias_ref, alpha, x1_ref, gates_ref):
    c = _layer_norm(conv_pre + cvec_ref[0:1, :], cvec_ref[1:2, :], cvec_ref[2:3, :])
    y_conv = jnp.dot(_silu(c).astype(BF16), convpw_ref[...], preferred_element_type=F32)
    cat = jnp.concatenate([y_att.astype(BF16), y_pool.astype(BF16), y_conv.astype(BF16)], axis=1)
    mix = jnp.dot(cat, wout_ref[...], preferred_element_type=F32)
    x1 = _layer_norm(alpha * x + mix, ln1_ref[0:1, :], ln1_ref[1:2, :])
    x1_ref[...] = x1

    xh = x1.astype(BF16)
    xl = (x1 - xh.astype(F32)).astype(BF16)
    lhs = jnp.concatenate([xh, xl, xh], axis=1)
    logits_t = jnp.dot(lhs, rw_ref[...], preferred_element_type=F32).T
    e_rows = [logits_t[e:e + 1, :] for e in range(N_EXPERTS)]
    mx = functools.reduce(jnp.maximum, e_rows)
    ex = [jnp.exp(r - mx) for r in e_rows]
    inv_den = 1.0 / functools.reduce(jnp.add, ex)
    probs = [r * inv_den for r in ex]
    sel = [probs[e] + rbias_ref[e:e + 1, :] for e in range(N_EXPERTS)]
    gscore = []
    for g in range(N_EXPERT_GROUPS):
        members = sel[g * EXPERTS_PER_GROUP:(g + 1) * EXPERTS_PER_GROUP]
        pair_sums = [members[i] + members[j] for i in range(EXPERTS_PER_GROUP)
                     for j in range(i + 1, EXPERTS_PER_GROUP)]
        gscore.append(functools.reduce(jnp.maximum, pair_sums))
    best_score = functools.reduce(jnp.maximum, gscore)
    taken = jnp.zeros(best_score.shape, jnp.bool_)
    chosen = []
    for g in range(N_EXPERT_GROUPS):
        is_best = (gscore[g] == best_score) & jnp.logical_not(taken)
        taken = taken | is_best
        members = sel[g * EXPERTS_PER_GROUP:(g + 1) * EXPERTS_PER_GROUP]
        for i in range(EXPERTS_PER_GROUP):
            rank = jnp.zeros(best_score.shape, jnp.int32)
            for j in range(EXPERTS_PER_GROUP):
                if j == i:
                    continue
                ahead = (members[j] >= members[i]) if j < i else (members[j] > members[i])
                rank = rank + ahead.astype(jnp.int32)
            chosen.append(is_best & (rank < EXPERT_TOPK))
    picked = [jnp.where(chosen[e], probs[e], 0.0) for e in range(N_EXPERTS)]
    inv_tot = 1.0 / functools.reduce(jnp.add, picked)
    for e in range(N_EXPERTS):
        gates_ref[e:e + 1, :] = picked[e] * inv_tot


def _pool_lane_select(per_window):
    width = per_window[0].shape[-1]
    gdim = width // POOL_GROUPS
    lane = lax.broadcasted_iota(jnp.int32, per_window[0].shape, 1)
    out = per_window[-1]
    for g in range(POOL_GROUPS - 2, -1, -1):
        out = jnp.where(lane < (g + 1) * gdim, per_window[g], out)
    return out


def _mix_prompt_kernel(yatt_ref, up_ref, uph_ref, glu_ref, gluh_ref, x_ref,
                       poolw_ref, pscale_ref, convw_ref, cvec_ref, convpw_ref, wout_ref, ln1_ref,
                       rw_ref, rbias_ref, x1_ref, gates_ref, pext_ref, cext_ref, cpre_ref,
                       *, tt, tiles_per_seq, alpha):
    i = pl.program_id(0)
    first = (i % tiles_per_seq) == 0
    row0 = (i % tiles_per_seq) * tt
    u = up_ref[...]
    pext_ref[0:POOL_HALO, :] = jnp.where(first, 0.0, uph_ref[...])
    pext_ref[POOL_HALO:, :] = u
    win_sums = []
    cur = u
    for back in range(1, max(POOL_WINDOWS)):
        cur = cur + pext_ref[POOL_HALO - back:POOL_HALO - back + tt, :]
        if back + 1 in POOL_WINDOWS:
            win_sums.append(cur)
    pos1 = (row0 + lax.broadcasted_iota(jnp.int32, (tt, 1), 0) + 1).astype(F32)
    pooled = _pool_lane_select([ws / jnp.minimum(pos1, float(wn)) for ws, wn in zip(win_sums, POOL_WINDOWS)])
    y_pool = jnp.dot((pooled - u).astype(BF16), poolw_ref[...], preferred_element_type=F32) * pscale_ref[...]

    cext_ref[0:CONV_HALO, :] = jnp.where(first, 0.0, gluh_ref[...])
    cext_ref[CONV_HALO:, :] = glu_ref[...]
    lead = CONV_HALO - CONV_HIST
    for c0 in range(0, tt, CONV_ROWS):
        acc = jnp.zeros((CONV_ROWS, glu_ref.shape[1]), F32)
        for j in range(CONV_KERNEL):
            acc = acc + cext_ref[c0 + lead + j:c0 + lead + j + CONV_ROWS, :] * convw_ref[j:j + 1, :]
        cpre_ref[c0:c0 + CONV_ROWS, :] = acc

    _mix_tail(yatt_ref[...], y_pool, cpre_ref[...], x_ref[...], cvec_ref, convpw_ref, wout_ref,
              ln1_ref, rw_ref, rbias_ref, alpha, x1_ref, gates_ref)


def _mix_decode_kernel(yatt_ref, up_ref, pstate_ref, glu_ref, cstate_ref, x_ref,
                       poolw_ref, pscale_ref, convw_ref, cvec_ref, convpw_ref, wout_ref, ln1_ref,
                       rw_ref, rbias_ref, x1_ref, gates_ref, *, past_len, alpha):
    u = up_ref[...]
    ps = pstate_ref[...]
    per_window = []
    for wn in POOL_WINDOWS:
        tot = u + jnp.sum(ps[:, POOL_HIST - (wn - 1):, :], axis=1)
        per_window.append(tot / float(min(past_len + 1, wn)))
    pooled = _pool_lane_select(per_window)
    y_pool = jnp.dot((pooled - u).astype(BF16), poolw_ref[...], preferred_element_type=F32) * pscale_ref[...]
    cw = convw_ref[...]
    conv_pre = (jnp.sum(cstate_ref[...] * cw[None, :CONV_HIST, :], axis=1)
                + glu_ref[...] * cw[CONV_HIST:CONV_KERNEL, :])
    _mix_tail(yatt_ref[...], y_pool, conv_pre, x_ref[...], cvec_ref, convpw_ref, wout_ref,
              ln1_ref, rw_ref, rbias_ref, alpha, x1_ref, gates_ref)


def _mix_weights(lw):
    pool_w = lw["pool_w"]
    g, gd, _ = pool_w.shape
    blockdiag = jnp.zeros((g * gd, g * gd), F32)
    for i in range(g):
        blockdiag = blockdiag.at[i * gd:(i + 1) * gd, i * gd:(i + 1) * gd].set(pool_w[i])
    rw = lw["router_w"]
    rw_h = rw.astype(BF16)
    rw_l = (rw - rw_h.astype(F32)).astype(BF16)
    rw3 = jnp.pad(jnp.concatenate([rw_h, rw_h, rw_l], axis=0), ((0, 0), (0, LANES - N_EXPERTS)))
    return dict(
        poolw=blockdiag.astype(BF16),
        pscale=lw["pool_scale"].reshape(1, -1),
        convw=lw["conv_w"],
        cvec=jnp.stack([lw["conv_b"], lw["conv_ln_g"], lw["conv_ln_b"]]),
        convpw=lw["conv_pw"].astype(BF16),
        wout=lw["w_out"].astype(BF16),
        ln1=jnp.stack([lw["ln1_g"], lw["ln1_b"]]),
        rw=rw3,
        rbias=lw["router_bias"].reshape(N_EXPERTS, 1),
    )


_MIX_W_ORDER = ("poolw", "pscale", "convw", "cvec", "convpw", "wout", "ln1", "rw", "rbias")


def _full_spec(a):
    nd = a.ndim
    return pl.BlockSpec(a.shape, lambda *_: (0,) * nd)


def _mix_prompt(y_att, u_pool, glu, x, mw, *, t, alpha):
    m, d = x.shape
    tt = min(TT_MIX, t)
    assert t % tt == 0 and tt % CONV_ROWS == 0 and tt % CONV_HALO == 0
    tiles_per_seq = t // tt
    pw, cw = u_pool.shape[1], glu.shape[1]
    row = lambda width: pl.BlockSpec((tt, width), lambda i: (i, 0))
    halo = lambda rows, width: pl.BlockSpec(
        (rows, width), lambda i: (jnp.maximum(i * (tt // rows) - 1, 0), 0))
    weights = [mw[k] for k in _MIX_W_ORDER]
    kernel = functools.partial(_mix_prompt_kernel, tt=tt, tiles_per_seq=tiles_per_seq, alpha=alpha)
    return pl.pallas_call(
        kernel,
        out_shape=[jax.ShapeDtypeStruct((m, d), F32), jax.ShapeDtypeStruct((N_EXPERTS, m), F32)],
        grid=(m // tt,),
        in_specs=[row(y_att.shape[1]), row(pw), halo(POOL_HALO, pw), row(cw), halo(CONV_HALO, cw), row(d)]
                 + [_full_spec(w) for w in weights],
        out_specs=[row(d), pl.BlockSpec((N_EXPERTS, tt), lambda i: (0, i))],
        scratch_shapes=[pltpu.VMEM((POOL_HALO + tt, pw), F32), pltpu.VMEM((CONV_HALO + tt, cw), F32),
                        pltpu.VMEM((tt, cw), F32)],
        compiler_params=_params("parallel"),
        name="mix_prompt",
    )(y_att, u_pool, u_pool, glu, glu, x, *weights)


def _mix_decode(y_att, u_pool, pool_state, glu, conv_state, x, mw, *, past_len, alpha):
    m, d = x.shape
    weights = [mw[k] for k in _MIX_W_ORDER]
    ins = [y_att, u_pool, pool_state, glu, conv_state, x] + weights
    kernel = functools.partial(_mix_decode_kernel, past_len=past_len, alpha=alpha)
    return pl.pallas_call(
        kernel,
        out_shape=[jax.ShapeDtypeStruct((m, d), F32), jax.ShapeDtypeStruct((N_EXPERTS, m), F32)],
        grid=(1,),
        in_specs=[_full_spec(a) for a in ins],
        out_specs=[pl.BlockSpec((m, d), lambda i: (0, 0)), pl.BlockSpec((N_EXPERTS, m), lambda i: (0, 0))],
        compiler_params=_params("arbitrary"),
        name="mix_decode",
    )(*ins)


def _moe_kernel(x_ref, gates_ref, wg_ref, wu_ref, wd_ref, ln_ref, o_ref, xb_ref, acc_ref, *, alpha):
    e = pl.program_id(1)

    @pl.when(e == 0)
    def _():
        xb_ref[...] = x_ref[...].astype(BF16)
        acc_ref[...] = jnp.zeros(acc_ref.shape, F32)

    xb = xb_ref[...]
    hg = jnp.dot(xb, wg_ref[0].astype(BF16), preferred_element_type=F32)
    hu = jnp.dot(xb, wu_ref[0].astype(BF16), preferred_element_type=F32)
    gates = gates_ref[...]
    lane = lax.broadcasted_iota(jnp.int32, gates.shape, 1)
    gate_e = jnp.sum(jnp.where(lane == e, gates, 0.0), axis=1, keepdims=True)
    hidden = (_silu(hg) * hu * gate_e).astype(BF16)
    acc_ref[...] += jnp.dot(hidden, wd_ref[0].astype(BF16), preferred_element_type=F32)

    @pl.when(e == pl.num_programs(1) - 1)
    def _():
        o_ref[...] = _layer_norm(alpha * x_ref[...] + acc_ref[...], ln_ref[0:1, :], ln_ref[1:2, :])


def _moe(x, gates, exp_gate, exp_up, exp_down, ln2, *, layer, alpha):
    m, d = x.shape
    tm = min(TM_MOE, m)
    assert m % tm == 0
    de = exp_gate.shape[-1]
    return pl.pallas_call(
        functools.partial(_moe_kernel, alpha=alpha),
        out_shape=jax.ShapeDtypeStruct((m, d), F32),
        grid=(m // tm, N_EXPERTS),
        in_specs=[
            pl.BlockSpec((tm, d), lambda i, e: (i, 0)),
            pl.BlockSpec((tm, N_EXPERTS), lambda i, e: (i, 0)),
            pl.BlockSpec((None, 1, d, de), lambda i, e: (layer, e, 0, 0)),
            pl.BlockSpec((None, 1, d, de), lambda i, e: (layer, e, 0, 0)),
            pl.BlockSpec((None, 1, de, d), lambda i, e: (layer, e, 0, 0)),
            pl.BlockSpec((2, d), lambda i, e: (0, 0)),
        ],
        out_specs=pl.BlockSpec((tm, d), lambda i, e: (i, 0)),
        scratch_shapes=[pltpu.VMEM((tm, d), BF16), pltpu.VMEM((tm, d), F32)],
        compiler_params=_params("parallel", "arbitrary"),
        name="moe",
    )(x, gates, exp_gate, exp_up, exp_down, ln2)


def _moe_plan(gates_t, tm, chunk):
    m = gates_t.shape[1]
    tiles = m // tm
    gsum = gates_t.reshape(N_EXPERT_GROUPS, EXPERTS_PER_GROUP, m).sum(axis=1)
    gid = jnp.argmax(gsum, axis=0).astype(jnp.int32).reshape(tiles, tm)
    onehot = (gid[:, None, :] == jnp.arange(N_EXPERT_GROUPS, dtype=jnp.int32)[None, :, None]).astype(jnp.int32)
    csum = jnp.cumsum(onehot, axis=2)
    nch = (csum[:, :, -1] + chunk - 1) // chunk
    start = (jnp.cumsum(nch, axis=1) - nch) * chunk
    dst = jnp.sum(onehot * (start[:, :, None] + csum - 1), axis=1)
    return dst.reshape(-1), start.reshape(-1), nch.reshape(-1)


def _moe_grouped_kernel(dst_ref, start_ref, nch_ref, x_ref, gates_ref, wg_ref, wu_ref, wd_ref, ln_ref,
                        o_ref, xs_ref, gs_ref, acc_ref, *, alpha, tm, chunk):
    i = pl.program_id(0)
    e = pl.program_id(1)
    half = x_ref.shape[1] // 2

    @pl.when(e == 0)
    def _():
        def pack(c, carry):
            r0 = pl.multiple_of(c * chunk, chunk)
            lo = pltpu.bitcast(x_ref[pl.ds(r0, chunk), :half].astype(BF16).astype(F32), jnp.uint32)
            hi = pltpu.bitcast(x_ref[pl.ds(r0, chunk), half:].astype(BF16).astype(F32), jnp.uint32)
            o_ref[pl.ds(r0, chunk), :half] = pltpu.bitcast((hi & jnp.uint32(0xFFFF0000)) | (lo >> 16), F32)
            return carry

        lax.fori_loop(0, tm // chunk, pack, 0)
        xs_ref[...] = jnp.zeros(xs_ref.shape, F32)
        gs_ref[...] = jnp.zeros(gs_ref.shape, F32)
        acc_ref[...] = jnp.zeros(acc_ref.shape, F32)

        def put(r, carry):
            d = dst_ref[i * tm + r]
            xs_ref[pl.ds(d, 1), :] = o_ref[pl.ds(r, 1), :half]
            gs_ref[pl.ds(d, 1), :] = gates_ref[pl.ds(r, 1), :]
            return carry

        lax.fori_loop(0, tm, put, 0, unroll=8)

    g = e // EXPERTS_PER_GROUP
    start = start_ref[i * N_EXPERT_GROUPS + g]
    wg = wg_ref[0].astype(BF16)
    wu = wu_ref[0].astype(BF16)
    wd = wd_ref[0].astype(BF16)
    lane = lax.broadcasted_iota(jnp.int32, (chunk, gs_ref.shape[1]), 1)

    def rows_of(c, carry):
        r0 = pl.multiple_of(start + c * chunk, chunk)
        words = pltpu.bitcast(xs_ref[pl.ds(r0, chunk), :], jnp.uint32)
        x_lo = pltpu.bitcast(words << 16, F32).astype(BF16)
        x_hi = pltpu.bitcast(words & jnp.uint32(0xFFFF0000), F32).astype(BF16)
        hg = (jnp.dot(x_lo, wg[:half], preferred_element_type=F32)
              + jnp.dot(x_hi, wg[half:], preferred_element_type=F32))
        hu = (jnp.dot(x_lo, wu[:half], preferred_element_type=F32)
              + jnp.dot(x_hi, wu[half:], preferred_element_type=F32))
        gate_e = jnp.sum(jnp.where(lane == e, gs_ref[pl.ds(r0, chunk), :], 0.0), axis=1, keepdims=True)
        hidden = (_silu(hg) * hu * gate_e).astype(BF16)
        acc_ref[pl.ds(r0, chunk), :] += jnp.dot(hidden, wd, preferred_element_type=F32)
        return carry

    lax.fori_loop(0, nch_ref[i * N_EXPERT_GROUPS + g], rows_of, 0)

    @pl.when(e == pl.num_programs(1) - 1)
    def _():
        def take(r, carry):
            d = dst_ref[i * tm + r]
            o_ref[pl.ds(r, 1), :] = acc_ref[pl.ds(d, 1), :]
            return carry

        lax.fori_loop(0, tm, take, 0, unroll=8)

        def norm(c, carry):
            rows = pl.ds(pl.multiple_of(c * chunk, chunk), chunk)
            o_ref[rows, :] = _layer_norm(alpha * x_ref[rows, :] + o_ref[rows, :], ln_ref[0:1, :], ln_ref[1:2, :])
            return carry

        lax.fori_loop(0, tm // chunk, norm, 0)


def _moe_grouped(x, gates_t, exp_gate, exp_up, exp_down, ln2, *, layer, alpha):
    m, d = x.shape
    tm, chunk = TM_MOE_GROUPED, MOE_CHUNK
    assert m % tm == 0
    de = exp_gate.shape[-1]
    sort_rows = tm + N_EXPERT_GROUPS * chunk
    dst, start, nch = _moe_plan(gates_t, tm, chunk)
    once = pl.Buffered(1)
    return pl.pallas_call(
        functools.partial(_moe_grouped_kernel, alpha=alpha, tm=tm, chunk=chunk),
        out_shape=jax.ShapeDtypeStruct((m, d), F32),
        grid_spec=pltpu.PrefetchScalarGridSpec(
            num_scalar_prefetch=3,
            grid=(m // tm, N_EXPERTS),
            in_specs=[
                pl.BlockSpec((tm, d), lambda i, e, *_: (i, 0), pipeline_mode=once),
                pl.BlockSpec((tm, N_EXPERTS), lambda i, e, *_: (i, 0), pipeline_mode=once),
                pl.BlockSpec((None, 1, d, de), lambda i, e, *_: (layer, e, 0, 0)),
                pl.BlockSpec((None, 1, d, de), lambda i, e, *_: (layer, e, 0, 0)),
                pl.BlockSpec((None, 1, de, d), lambda i, e, *_: (layer, e, 0, 0)),
                pl.BlockSpec((2, d), lambda i, e, *_: (0, 0)),
            ],
            out_specs=pl.BlockSpec((tm, d), lambda i, e, *_: (i, 0), pipeline_mode=once),
            scratch_shapes=[pltpu.VMEM((sort_rows, d // 2), F32), pltpu.VMEM((sort_rows, N_EXPERTS), F32),
                            pltpu.VMEM((sort_rows, d), F32)],
        ),
        compiler_params=pltpu.CompilerParams(dimension_semantics=("parallel", "arbitrary"),
                                             vmem_limit_bytes=VMEM_LIMIT_MOE),
        name="moe_grouped",
    )(dst, start, nch, x, gates_t.T, exp_gate, exp_up, exp_down, ln2)


def kernel(x_prompt, x_sample, cache_k, cache_v, state_pool, state_conv, page_table, w_in, pool_w, pool_scale, conv_w, conv_b, conv_ln_g, conv_ln_b, conv_pw, w_out, ln1_g, ln1_b, router_w, router_bias, exp_gate, exp_up, exp_down, ln2_g, ln2_b):
    b, t, d = x_prompt.shape
    bd, td, _ = x_sample.shape
    assert td == 1
    depth = w_in.shape[0]
    alpha = (2 * depth) ** 0.25
    past_len = page_table.shape[1] * cache_k.shape[2]

    hp = x_prompt.reshape(b * t, d)
    hs = x_sample.reshape(bd, d)
    ks_l, vs_l, pp_l, ps_l, cp_l, cs_l = [], [], [], [], [], []
    kv_all = None
    for l in range(depth):
        lw = dict(pool_w=pool_w[l], pool_scale=pool_scale[l], conv_w=conv_w[l], conv_b=conv_b[l],
                  conv_ln_g=conv_ln_g[l], conv_ln_b=conv_ln_b[l], conv_pw=conv_pw[l], w_out=w_out[l],
                  ln1_g=ln1_g[l], ln1_b=ln1_b[l], router_w=router_w, router_bias=router_bias)
        mw = _mix_weights(lw)
        w_in_bf = w_in[l].astype(BF16)
        wqkvt_bf = w_in_bf[:, :3 * ATT_WIDTH].T
        ln2 = jnp.stack([ln2_g[l], ln2_b[l]])

        qt, kb, kmean, vtb, u_pool, glu, *kv_all = _in_proj_prompt(hp, w_in_bf, wqkvt_bf, kv_all, b=b, t=t,
                                                                    layer=l, depth=depth)
        y_att = _moba_prompt(qt, kb, vtb, kmean, b, t)
        x1, gates_t = _mix_prompt(y_att, u_pool, glu, hp, mw, t=t, alpha=alpha)
        hp = _moe_grouped(x1, gates_t, exp_gate, exp_up, exp_down, ln2, layer=l, alpha=alpha)
        pp_l.append(u_pool.reshape(b, t, -1)[:, t - POOL_HIST:])
        cp_l.append(glu.reshape(b, t, -1)[:, t - CONV_HIST:])

        qs, ks, vs, us, gs = _in_proj_decode(hs, w_in_bf)
        ys = _moba_decode(qs, ks, vs, cache_k, cache_v, page_table, l)
        x1s, gates_ts = _mix_decode(ys, us, state_pool[:, l], gs, state_conv[:, l], hs, mw,
                                    past_len=past_len, alpha=alpha)
        hs = _moe(x1s, gates_ts.T, exp_gate, exp_up, exp_down, ln2, layer=l, alpha=alpha)
        ks_l.append(ks.reshape(bd, 1, ATT_HEADS, HEAD_DIM))
        vs_l.append(vs.reshape(bd, 1, ATT_HEADS, HEAD_DIM))
        ps_l.append(jnp.concatenate([state_pool[:, l, 1:], us[:, None, :]], axis=1))
        cs_l.append(jnp.concatenate([state_conv[:, l, 1:], gs[:, None, :]], axis=1))

    stack = lambda xs: jnp.stack(xs, axis=1)

    def heads_last(x):
        return jnp.transpose(x.reshape(b, depth, ATT_HEADS, HEAD_DIM, t), (0, 1, 4, 2, 3))

    return (hp.reshape(b, t, d), hs.reshape(bd, 1, d), heads_last(kv_all[0]), heads_last(kv_all[1]),
            stack(ks_l), stack(vs_l),
            stack(pp_l), stack(ps_l), stack(cp_l), stack(cs_l))
```

```python
import functools
import math

import jax
import jax.numpy as jnp
from jax import lax
from jax.experimental import pallas as pl
from jax.experimental.pallas import tpu as pltpu

F32 = jnp.float32
BF16 = jnp.bfloat16

ATT_HEADS = 8
HEAD_DIM = 64
ATT_WIDTH = ATT_HEADS * HEAD_DIM
POOL_WINDOWS = (2, 4, 8, 16)
POOL_GROUPS = len(POOL_WINDOWS)
POOL_HIST = max(POOL_WINDOWS) - 1
CONV_KERNEL = 31
CONV_HIST = CONV_KERNEL - 1
MOBA_BLOCK = 256
MOBA_TOPK = 3
N_EXPERTS = 16
N_EXPERT_GROUPS = 4
EXPERTS_PER_GROUP = N_EXPERTS // N_EXPERT_GROUPS
EXPERT_TOPK = 2
LN_EPS = 1e-5
NEG_BIG = -1e30

LOG2_E = math.log2(math.e)
LANES = 128
SUBLANES = 8
HEADS_PER_TILE = LANES // HEAD_DIM
BF16_ROWS = 2 * SUBLANES
V_ROWS = HEAD_DIM + BF16_ROWS
POOL_HALO = 16
CONV_HALO = 32
VMEM_LIMIT = 56 * 1024 * 1024

TM_IN = 512
TT_MIX = 512
TM_MOE = 1024
TM_MOE_GROUPED = 2048
MOE_CHUNK = 256
VMEM_LIMIT_MOE = 60 * 1024 * 1024
CONV_ROWS = 64


def _params(*sem):
    return pltpu.CompilerParams(dimension_semantics=sem, vmem_limit_bytes=VMEM_LIMIT)


def _sigmoid(x):
    return 1.0 / (1.0 + jnp.exp(-x))


def _silu(x):
    return x * _sigmoid(x)


def _layer_norm(x, g, b):
    mu = jnp.mean(x, axis=-1, keepdims=True)
    xc = x - mu
    var = jnp.mean(xc * xc, axis=-1, keepdims=True)
    return xc * lax.rsqrt(var + LN_EPS) * g + b


def _alibi_slopes(n_heads):
    return 2.0 ** (-8.0 * jnp.arange(1, n_heads + 1, dtype=F32) / n_heads)


def _proj_segments(xb, w_ref, pw, cw):
    aw = ATT_WIDTH

    def seg(lo, width):
        return jnp.dot(xb, w_ref[:, lo:lo + width], preferred_element_type=F32)

    glu = seg(3 * aw + pw, cw) * _sigmoid(seg(3 * aw + pw + cw, cw))
    return seg, seg(3 * aw, pw), glu


def _in_proj_decode_kernel(x_ref, w_ref, q_ref, k_ref, v_ref, up_ref, glu_ref):
    xb = x_ref[...].astype(BF16)
    seg, up, glu = _proj_segments(xb, w_ref, up_ref.shape[-1], glu_ref.shape[-1])
    q_ref[...] = seg(0, ATT_WIDTH)
    k_ref[...] = seg(ATT_WIDTH, ATT_WIDTH)
    v_ref[...] = seg(2 * ATT_WIDTH, ATT_WIDTH)
    up_ref[...] = up
    glu_ref[...] = glu


def _in_proj_prompt_kernel(x_ref, w_ref, wqkvt_ref, *refs, tm):
    qt_ref, kb_ref, km_ref, vtb_ref, up_ref, glu_ref, kt_ref, vt_ref = refs[-8:]
    aw = ATT_WIDTH
    blk = MOBA_BLOCK
    xb = x_ref[...].astype(BF16)
    seg, up, glu = _proj_segments(xb, w_ref, up_ref.shape[-1], glu_ref.shape[-1])
    up_ref[...] = up
    glu_ref[...] = glu
    k = seg(aw, aw)
    kb_ref[...] = k.astype(BF16)
    for i in range(tm // blk):
        km_ref[i] = jnp.mean(k[i * blk:(i + 1) * blk], axis=0, keepdims=True)
    qkvt = lax.dot_general(wqkvt_ref[...], xb, (((1,), (1,)), ((), ())), preferred_element_type=F32)
    qt_ref[...] = qkvt[:aw]
    kt_ref[...] = qkvt[aw:2 * aw]
    vt = qkvt[2 * aw:]
    vt_ref[...] = vt
    tail_row = lax.broadcasted_iota(jnp.int32, (V_ROWS - HEAD_DIM, blk), 0)
    tail = jnp.where(tail_row == 0, 1.0, 0.0).astype(BF16)
    for i in range(tm // blk):
        vb = vt[:, i * blk:(i + 1) * blk].astype(BF16)
        vtb_ref[i] = jnp.concatenate(
            [piece for h in range(ATT_HEADS) for piece in (vb[h * HEAD_DIM:(h + 1) * HEAD_DIM], tail)], axis=0)


def _split_widths(n):
    pw = (n - 3 * ATT_WIDTH) // 3
    return pw, pw


def _in_proj_decode(x, w_bf):
    m, d = x.shape
    n = w_bf.shape[1]
    pw, cw = _split_widths(n)
    full = lambda width: pl.BlockSpec((m, width), lambda i: (0, 0))
    return pl.pallas_call(
        _in_proj_decode_kernel,
        out_shape=[jax.ShapeDtypeStruct((m, ATT_WIDTH), F32)] * 3 + [
            jax.ShapeDtypeStruct((m, pw), F32), jax.ShapeDtypeStruct((m, cw), F32)],
        grid=(1,),
        in_specs=[full(d), pl.BlockSpec((d, n), lambda i: (0, 0))],
        out_specs=[full(ATT_WIDTH)] * 3 + [full(pw), full(cw)],
        compiler_params=_params("arbitrary"),
        name="in_proj_decode",
    )(x, w_bf)


def _in_proj_prompt(x, w_bf, wqkvt_bf, kv_all, *, b, t, layer, depth):
    m, d = x.shape
    n = w_bf.shape[1]
    aw = ATT_WIDTH
    blk = MOBA_BLOCK
    pw, cw = _split_widths(n)
    tm = min(TM_IN, t)
    assert t % tm == 0 and tm % blk == 0
    tps = t // tm
    nblk = tm // blk
    row = lambda width: pl.BlockSpec((tm, width), lambda i: (i, 0))
    tok_lanes = pl.BlockSpec((None, aw, tm), lambda i: (i // tps, 0, i % tps))
    layer_slab = pl.BlockSpec((None, None, aw, tm), lambda i: (i // tps, layer, 0, i % tps))
    all_layers = jax.ShapeDtypeStruct((b, depth, aw, t), F32)
    carried = [] if kv_all is None else list(kv_all)
    n_fixed = 3
    return pl.pallas_call(
        functools.partial(_in_proj_prompt_kernel, tm=tm),
        out_shape=[jax.ShapeDtypeStruct((b, aw, t), F32), jax.ShapeDtypeStruct((m, aw), BF16),
                   jax.ShapeDtypeStruct((m // blk, 1, aw), F32),
                   jax.ShapeDtypeStruct((m // blk, ATT_HEADS * V_ROWS, blk), BF16),
                   jax.ShapeDtypeStruct((m, pw), F32), jax.ShapeDtypeStruct((m, cw), F32),
                   all_layers, all_layers],
        grid=(m // tm,),
        in_specs=[row(d), pl.BlockSpec((d, n), lambda i: (0, 0)), pl.BlockSpec((3 * aw, d), lambda i: (0, 0))]
                 + [pl.BlockSpec(memory_space=pl.ANY)] * len(carried),
        out_specs=[tok_lanes, row(aw), pl.BlockSpec((nblk, 1, aw), lambda i: (i, 0, 0)),
                   pl.BlockSpec((nblk, ATT_HEADS * V_ROWS, blk), lambda i: (i, 0, 0)), row(pw), row(cw),
                   layer_slab, layer_slab],
        input_output_aliases={n_fixed + k: 6 + k for k in range(len(carried))},
        compiler_params=_params("parallel"),
        name="in_proj_prompt",
    )(x, w_bf, wqkvt_bf, *carried)


def _moba_prompt_kernel(slopes_ref, q_ref, k_ref, vt_ref, km_ref, ab_ref,
                        o_ref, qt_ref, sel_ref, sa_ref, sb_ref, pa_ref, pb_ref, *, n_sel):
    blk = MOBA_BLOCK
    hp = pl.program_id(1)
    qi = pl.program_id(2)
    q_t = q_ref[...]
    km = km_ref[0]
    nbp = km.shape[0]
    scale = HEAD_DIM ** -0.5 * LOG2_E

    km_lane = lax.broadcasted_iota(jnp.int32, km.shape, 1)
    km_heads = jnp.concatenate(
        [jnp.where((km_lane >= hh * HEAD_DIM) & (km_lane < (hh + 1) * HEAD_DIM), km, 0.0)
         for hh in range(HEADS_PER_TILE)], axis=0)
    gates = jnp.dot(km_heads, q_t, precision=lax.Precision.HIGHEST, preferred_element_type=F32)
    row = lax.broadcasted_iota(jnp.int32, (nbp, blk), 0)
    row_f = row.astype(F32)
    dim = lax.broadcasted_iota(jnp.int32, (LANES, blk), 0)
    for hh in range(HEADS_PER_TILE):
        gate = jnp.where(row < qi, gates[hh * nbp:(hh + 1) * nbp], -jnp.inf)
        sel = jnp.zeros((nbp, blk), jnp.bool_)
        for _ in range(n_sel):
            mx = jnp.max(gate, axis=0, keepdims=True)
            first = jnp.min(jnp.where(gate == mx, row_f, float(nbp)), axis=0, keepdims=True)
            pick = (row_f == first) & (mx > -jnp.inf)
            sel = sel | pick
            gate = jnp.where(pick, -jnp.inf, gate)
        sel_ref[hh] = jnp.where(sel | (row == qi), 1.0, 0.0)
        in_head = (dim >= hh * HEAD_DIM) & (dim < (hh + 1) * HEAD_DIM)
        qt_ref[hh] = (jnp.where(in_head, q_t, 0.0) * scale).astype(BF16)

    slopes = [slopes_ref[hp * HEADS_PER_TILE + hh] for hh in range(HEADS_PER_TILE)]

    def scores(j, slot_ref):
        jc = jnp.minimum(j, qi)
        start = pl.multiple_of(jc * blk, blk)
        kj = k_ref[0, pl.ds(start, blk), :]
        own = (jc == qi).astype(jnp.int32)
        for hh in range(HEADS_PER_TILE):
            slot_ref[hh] = jnp.dot(kj, qt_ref[hh], preferred_element_type=F32) + ab_ref[own, hh]

    def softmax(j, slot_ref, p_ref, ms):
        jc = jnp.minimum(j, qi)
        dist = ((qi - jc) * blk).astype(F32)
        new_ms, alphas = [], []
        for hh in range(HEADS_PER_TILE):
            s = slot_ref[hh]
            picked = (sel_ref[hh, pl.ds(jc, 1), :] > 0.5) & (j <= qi)
            c = -(slopes[hh] * dist)
            m_new = jnp.maximum(ms[hh], jnp.max(s, axis=0, keepdims=True) + c)
            shift = jnp.where(picked, m_new - c, -NEG_BIG)
            m_new = jnp.where(picked, m_new, ms[hh])
            p_ref[hh] = jnp.exp2(s - shift).astype(BF16)
            alphas.append(jnp.exp2(ms[hh] - m_new))
            new_ms.append(m_new)
        return tuple(new_ms), tuple(alphas)

    def weighted_values(j, p_ref, alphas, accs):
        vtj = vt_ref[jnp.clip(j, 0, qi)]
        return tuple(accs[hh] * alphas[hh]
                     + jnp.dot(vtj[hh * V_ROWS:(hh + 1) * V_ROWS, :], p_ref[hh], preferred_element_type=F32)
                     for hh in range(HEADS_PER_TILE))

    def pair(jj, carry):
        ms, accs, pending = carry
        j0 = 2 * jj
        scores(j0 + 1, sb_ref)
        accs = weighted_values(j0 - 1, pb_ref, pending, accs)
        ms, alphas = softmax(j0, sa_ref, pa_ref, ms)
        scores(j0 + 2, sa_ref)
        accs = weighted_values(j0, pa_ref, alphas, accs)
        ms, pending = softmax(j0 + 1, sb_ref, pb_ref, ms)
        return ms, accs, pending

    heads = range(HEADS_PER_TILE)
    init = (tuple(jnp.full((1, blk), NEG_BIG, F32) for _ in heads),
            tuple(jnp.zeros((V_ROWS, blk), F32) for _ in heads),
            tuple(jnp.ones((1, blk), F32) for _ in heads))
    pb_ref[...] = jnp.zeros(pb_ref.shape, BF16)
    scores(jnp.int32(0), sa_ref)
    n_pairs = qi // 2 + 1
    ms, accs, pending = lax.fori_loop(0, n_pairs, pair, init)
    accs = weighted_values(2 * n_pairs - 1, pb_ref, pending, accs)
    out_t = jnp.concatenate([accs[hh][:HEAD_DIM] * (1.0 / accs[hh][HEAD_DIM:HEAD_DIM + 1])
                             for hh in range(HEADS_PER_TILE)], axis=0)
    o_ref[...] = out_t.T.astype(o_ref.dtype)


def _moba_prompt(qt, kb, vtb, kmean, b, t):
    blk = MOBA_BLOCK
    nb = t // blk
    assert t % blk == 0
    nbp = -(-nb // SUBLANES) * SUBLANES
    n_sel = min(MOBA_TOPK, nb)
    slopes = _alibi_slopes(ATT_HEADS) * LOG2_E
    r = jnp.arange(blk, dtype=F32)
    rel = r[None, :] - r[:, None]
    ab_off = -slopes[:, None, None] * rel[None]
    ab = jnp.stack([ab_off, jnp.where(rel[None] >= 0, ab_off, NEG_BIG)])
    km = jnp.pad(kmean.reshape(b, nb, ATT_WIDTH), ((0, 0), (0, nbp - nb), (0, 0)))
    kb3 = kb.reshape(b, t, ATT_WIDTH)
    n_hp = ATT_HEADS // HEADS_PER_TILE
    return pl.pallas_call(
        functools.partial(_moba_prompt_kernel, n_sel=n_sel),
        out_shape=jax.ShapeDtypeStruct((b * t, ATT_WIDTH), BF16),
        grid=(b, n_hp, nb),
        in_specs=[
            pl.BlockSpec(memory_space=pltpu.SMEM),
            pl.BlockSpec((None, LANES, blk), lambda bi, hp, qi: (bi, hp, qi)),
            pl.BlockSpec((1, t, LANES), lambda bi, hp, qi: (bi, 0, hp)),
            pl.BlockSpec((nb, HEADS_PER_TILE * V_ROWS, blk), lambda bi, hp, qi: (bi, hp, 0)),
            pl.BlockSpec((1, nbp, LANES), lambda bi, hp, qi: (bi, 0, hp)),
            pl.BlockSpec((2, HEADS_PER_TILE, blk, blk), lambda bi, hp, qi: (0, hp, 0, 0)),
        ],
        out_specs=pl.BlockSpec((blk, LANES), lambda bi, hp, qi: (bi * nb + qi, hp)),
        scratch_shapes=[pltpu.VMEM((HEADS_PER_TILE, LANES, blk), BF16),
                        pltpu.VMEM((HEADS_PER_TILE, nbp, blk), F32),
                        pltpu.VMEM((HEADS_PER_TILE, blk, blk), F32),
                        pltpu.VMEM((HEADS_PER_TILE, blk, blk), F32),
                        pltpu.VMEM((HEADS_PER_TILE, blk, blk), BF16),
                        pltpu.VMEM((HEADS_PER_TILE, blk, blk), BF16)],
        compiler_params=_params("parallel", "parallel", "arbitrary"),
        name="moba_prompt",
    )(slopes, qt, kb3, vtb, km, ab)


def _lane_bcast_cols(row):
    slabs = [jnp.broadcast_to(row[:, s * LANES:(s + 1) * LANES], (LANES, LANES)).T
             for s in range(ATT_WIDTH // LANES)]
    return jnp.concatenate(slabs, axis=0).reshape(ATT_HEADS, HEAD_DIM, LANES)


def _moba_decode_kernel(pt_ref, q_ref, kn_ref, vn_ref, slopes_ref, *rest,
                        pages_per_block, page, past_len, n_sel):
    n_pages = (len(rest) - 1) // 2
    kt_refs = rest[:n_pages]
    vt_refs = rest[n_pages:2 * n_pages]
    o_ref = rest[2 * n_pages]
    nb = n_pages // pages_per_block
    scale = HEAD_DIM ** -0.5
    qb = _lane_bcast_cols(q_ref[0])
    slopes = slopes_ref[...]
    lane_f = lax.broadcasted_iota(jnp.int32, (1, 1, page), 2).astype(F32)

    scores, gates, blk_max = [], [], []
    for i in range(nb):
        raw_sum, top = None, None
        for pg in range(i * pages_per_block, (i + 1) * pages_per_block):
            raw = jnp.sum(kt_refs[pg][...] * qb, axis=1, keepdims=True)
            s = raw * scale - slopes * (float(past_len - pg * page) - lane_f)
            scores.append(s)
            raw_sum = raw if raw_sum is None else raw_sum + raw
            top = s if top is None else jnp.maximum(top, s)
        gates.append(jnp.sum(raw_sum, axis=2, keepdims=True) * (1.0 / MOBA_BLOCK))
        blk_max.append(jnp.max(top, axis=2, keepdims=True))

    knb = _lane_bcast_cols(kn_ref[0])
    vnb = _lane_bcast_cols(vn_ref[0])
    s_own = jnp.max(jnp.sum(knb * qb, axis=1, keepdims=True), axis=2, keepdims=True) * scale
    m_tot = s_own
    sels = []
    for i in range(nb):
        rank = jnp.zeros(gates[i].shape, jnp.int32)
        for j in range(nb):
            if j == i:
                continue
            ahead = (gates[j] >= gates[i]) if j < i else (gates[j] > gates[i])
            rank = rank + ahead.astype(jnp.int32)
        sel = rank < n_sel
        sels.append(sel)
        m_tot = jnp.where(sel, jnp.maximum(m_tot, blk_max[i]), m_tot)

    w_own = jnp.exp(s_own - m_tot)
    lane = lax.broadcasted_iota(jnp.int32, vnb.shape, 2)
    num = jnp.where(lane == 0, w_own * vnb, 0.0)
    den_lanes = jnp.zeros(scores[0].shape, F32)
    for pg in range(n_pages):
        w = jnp.where(sels[pg // pages_per_block], jnp.exp(scores[pg] - m_tot), 0.0)
        den_lanes = den_lanes + w
        num = num + w * vt_refs[pg][...]
    den = w_own + jnp.sum(den_lanes, axis=2, keepdims=True)
    flat = (num * (1.0 / den)).reshape(ATT_WIDTH, LANES)
    pieces = [jnp.sum(flat[s * LANES:(s + 1) * LANES, :].T, axis=0, keepdims=True)
              for s in range(ATT_WIDTH // LANES)]
    o_ref[0] = jnp.concatenate(pieces, axis=1)


def _moba_decode(q, k_new, v_new, cache_k, cache_v, page_table, layer):
    bd = q.shape[0]
    n_pages = page_table.shape[1]
    page = cache_k.shape[2]
    past_len = n_pages * page
    assert MOBA_BLOCK % page == 0 and past_len % MOBA_BLOCK == 0 and page == LANES
    ppb = MOBA_BLOCK // page
    nb = past_len // MOBA_BLOCK
    n_sel = min(MOBA_TOPK, nb + 1)
    kt = jnp.transpose(cache_k, (0, 1, 3, 4, 2))
    vt = jnp.transpose(cache_v, (0, 1, 3, 4, 2))
    row3 = lambda a: a.reshape(bd, 1, ATT_WIDTH)
    slopes = _alibi_slopes(ATT_HEADS).reshape(ATT_HEADS, 1, 1)
    tok = pl.BlockSpec((1, 1, ATT_WIDTH), lambda bi, pt: (bi, 0, 0))

    def page_spec(pg):
        return pl.BlockSpec((None, None, ATT_HEADS, HEAD_DIM, page),
                            lambda bi, pt: (pt[bi * n_pages + pg], layer, 0, 0, 0))

    kernel = functools.partial(_moba_decode_kernel, pages_per_block=ppb, page=page,
                               past_len=past_len, n_sel=n_sel)
    out = pl.pallas_call(
        kernel,
        out_shape=jax.ShapeDtypeStruct((bd, 1, ATT_WIDTH), F32),
        grid_spec=pltpu.PrefetchScalarGridSpec(
            num_scalar_prefetch=1,
            grid=(bd,),
            in_specs=[tok, tok, tok, pl.BlockSpec((ATT_HEADS, 1, 1), lambda bi, pt: (0, 0, 0))]
                     + [page_spec(pg) for pg in range(n_pages)] * 2,
            out_specs=tok,
        ),
        compiler_params=_params("parallel"),
        name="moba_decode",
    )(page_table.reshape(-1), row3(q), row3(k_new), row3(v_new), slopes,
      *([kt] * n_pages), *([vt] * n_pages))
    return out.reshape(bd, ATT_WIDTH)


def _mix_tail(y_att, y_pool, conv_pre, x, cvec_ref, convpw_ref, wout_ref, ln1_ref,
              rw_ref, rbias_ref, alpha, x1_ref, gates_ref):
    c = _layer_norm(conv_pre + cvec_ref[0:1, :], cvec_ref[1:2, :], cvec_ref[2:3, :])
    y_conv = jnp.dot(_silu(c).astype(BF16), convpw_ref[...], preferred_element_type=F32)
    cat = jnp.concatenate([y_att.astype(BF16), y_pool.astype(BF16), y_conv.astype(BF16)], axis=1)
    mix = jnp.dot(cat, wout_ref[...], preferred_element_type=F32)
    x1 = _layer_norm(alpha * x + mix, ln1_ref[0:1, :], ln1_ref[1:2, :])
    x1_ref[...] = x1

    xh = x1.astype(BF16)
    xl = (x1 - xh.astype(F32)).astype(BF16)
    lhs = jnp.concatenate([xh, xl, xh], axis=1)
    logits_t = jnp.dot(lhs, rw_ref[...], preferred_element_type=F32).T
    e_rows = [logits_t[e:e + 1, :] for e in range(N_EXPERTS)]
    mx = functools.reduce(jnp.maximum, e_rows)
    ex = [jnp.exp(r - mx) for r in e_rows]
    inv_den = 1.0 / functools.reduce(jnp.add, ex)
    probs = [r * inv_den for r in ex]
    sel = [probs[e] + rbias_ref[e:e + 1, :] for e in range(N_EXPERTS)]
    gscore = []
    for g in range(N_EXPERT_GROUPS):
        members = sel[g * EXPERTS_PER_GROUP:(g + 1) * EXPERTS_PER_GROUP]
        pair_sums = [members[i] + members[j] for i in range(EXPERTS_PER_GROUP)
                     for j in range(i + 1, EXPERTS_PER_GROUP)]
        gscore.append(functools.reduce(jnp.maximum, pair_sums))
    best_score = functools.reduce(jnp.maximum, gscore)
    taken = jnp.zeros(best_score.shape, jnp.bool_)
    chosen = []
    for g in range(N_EXPERT_GROUPS):
        is_best = (gscore[g] == best_score) & jnp.logical_not(taken)
        taken = taken | is_best
        members = sel[g * EXPERTS_PER_GROUP:(g + 1) * EXPERTS_PER_GROUP]
        for i in range(EXPERTS_PER_GROUP):
            rank = jnp.zeros(best_score.shape, jnp.int32)
            for j in range(EXPERTS_PER_GROUP):
                if j == i:
                    continue
                ahead = (members[j] >= members[i]) if j < i else (members[j] > members[i])
                rank = rank + ahead.astype(jnp.int32)
            chosen.append(is_best & (rank < EXPERT_TOPK))
    picked = [jnp.where(chosen[e], probs[e], 0.0) for e in range(N_EXPERTS)]
    inv_tot = 1.0 / functools.reduce(jnp.add, picked)
    for e in range(N_EXPERTS):
        gates_ref[e:e + 1, :] = picked[e] * inv_tot


def _pool_lane_select(per_window):
    width = per_window[0].shape[-1]
    gdim = width // POOL_GROUPS
    lane = lax.broadcasted_iota(jnp.int32, per_window[0].shape, 1)
    out = per_window[-1]
    for g in range(POOL_GROUPS - 2, -1, -1):
        out = jnp.where(lane < (g + 1) * gdim, per_window[g], out)
    return out


def _mix_prompt_kernel(yatt_ref, up_ref, uph_ref, glu_ref, gluh_ref, x_ref,
                       poolw_ref, pscale_ref, convw_ref, cvec_ref, convpw_ref, wout_ref, ln1_ref,
                       rw_ref, rbias_ref, x1_ref, gates_ref, pext_ref, cext_ref, cpre_ref,
                       *, tt, tiles_per_seq, alpha):
    i = pl.program_id(0)
    first = (i % tiles_per_seq) == 0
    row0 = (i % tiles_per_seq) * tt
    u = up_ref[...]
    pext_ref[0:POOL_HALO, :] = jnp.where(first, 0.0, uph_ref[...])
    pext_ref[POOL_HALO:, :] = u
    win_sums = []
    cur = u
    for back in range(1, max(POOL_WINDOWS)):
        cur = cur + pext_ref[POOL_HALO - back:POOL_HALO - back + tt, :]
        if back + 1 in POOL_WINDOWS:
            win_sums.append(cur)
    pos1 = (row0 + lax.broadcasted_iota(jnp.int32, (tt, 1), 0) + 1).astype(F32)
    pooled = _pool_lane_select([ws / jnp.minimum(pos1, float(wn)) for ws, wn in zip(win_sums, POOL_WINDOWS)])
    y_pool = jnp.dot((pooled - u).astype(BF16), poolw_ref[...], preferred_element_type=F32) * pscale_ref[...]

    ext_len = CONV_HALO + tt
    cext_ref[0, 0:CONV_HALO, :] = jnp.where(first, 0.0, gluh_ref[...])
    cext_ref[0, CONV_HALO:, :] = glu_ref[...]
    for o in range(1, SUBLANES):
        cext_ref[o, 0:ext_len - SUBLANES, :] = cext_ref[0, o:o + ext_len - SUBLANES, :]
    lead = CONV_HALO - CONV_HIST
    for c0 in range(0, tt, CONV_ROWS):
        acc = jnp.zeros((CONV_ROWS, glu_ref.shape[1]), F32)
        for j in range(CONV_KERNEL):
            o, a = (lead + j) % SUBLANES, (lead + j) // SUBLANES * SUBLANES
            acc = acc + cext_ref[o, c0 + a:c0 + a + CONV_ROWS, :] * convw_ref[j:j + 1, :]
        cpre_ref[c0:c0 + CONV_ROWS, :] = acc

    _mix_tail(yatt_ref[...], y_pool, cpre_ref[...], x_ref[...], cvec_ref, convpw_ref, wout_ref,
              ln1_ref, rw_ref, rbias_ref, alpha, x1_ref, gates_ref)


def _mix_decode_kernel(yatt_ref, up_ref, pstate_ref, glu_ref, cstate_ref, x_ref,
                       poolw_ref, pscale_ref, convw_ref, cvec_ref, convpw_ref, wout_ref, ln1_ref,
                       rw_ref, rbias_ref, x1_ref, gates_ref, *, past_len, alpha):
    u = up_ref[...]
    ps = pstate_ref[...]
    per_window = []
    for wn in POOL_WINDOWS:
        tot = u + jnp.sum(ps[:, POOL_HIST - (wn - 1):, :], axis=1)
        per_window.append(tot / float(min(past_len + 1, wn)))
    pooled = _pool_lane_select(per_window)
    y_pool = jnp.dot((pooled - u).astype(BF16), poolw_ref[...], preferred_element_type=F32) * pscale_ref[...]
    cw = convw_ref[...]
    conv_pre = (jnp.sum(cstate_ref[...] * cw[None, :CONV_HIST, :], axis=1)
                + glu_ref[...] * cw[CONV_HIST:CONV_KERNEL, :])
    _mix_tail(yatt_ref[...], y_pool, conv_pre, x_ref[...], cvec_ref, convpw_ref, wout_ref,
              ln1_ref, rw_ref, rbias_ref, alpha, x1_ref, gates_ref)


def _mix_weights(lw):
    pool_w = lw["pool_w"]
    g, gd, _ = pool_w.shape
    blockdiag = jnp.zeros((g * gd, g * gd), F32)
    for i in range(g):
        blockdiag = blockdiag.at[i * gd:(i + 1) * gd, i * gd:(i + 1) * gd].set(pool_w[i])
    rw = lw["router_w"]
    rw_h = rw.astype(BF16)
    rw_l = (rw - rw_h.astype(F32)).astype(BF16)
    rw3 = jnp.pad(jnp.concatenate([rw_h, rw_h, rw_l], axis=0), ((0, 0), (0, LANES - N_EXPERTS)))
    return dict(
        poolw=blockdiag.astype(BF16),
        pscale=lw["pool_scale"].reshape(1, -1),
        convw=lw["conv_w"],
        cvec=jnp.stack([lw["conv_b"], lw["conv_ln_g"], lw["conv_ln_b"]]),
        convpw=lw["conv_pw"].astype(BF16),
        wout=lw["w_out"].astype(BF16),
        ln1=jnp.stack([lw["ln1_g"], lw["ln1_b"]]),
        rw=rw3,
        rbias=lw["router_bias"].reshape(N_EXPERTS, 1),
    )


_MIX_W_ORDER = ("poolw", "pscale", "convw", "cvec", "convpw", "wout", "ln1", "rw", "rbias")


def _full_spec(a):
    nd = a.ndim
    return pl.BlockSpec(a.shape, lambda *_: (0,) * nd)


def _mix_prompt(y_att, u_pool, glu, x, mw, *, t, alpha):
    m, d = x.shape
    tt = min(TT_MIX, t)
    assert t % tt == 0 and tt % CONV_ROWS == 0 and tt % CONV_HALO == 0
    tiles_per_seq = t // tt
    pw, cw = u_pool.shape[1], glu.shape[1]
    row = lambda width: pl.BlockSpec((tt, width), lambda i: (i, 0))
    halo = lambda rows, width: pl.BlockSpec(
        (rows, width), lambda i: (jnp.maximum(i * (tt // rows) - 1, 0), 0))
    weights = [mw[k] for k in _MIX_W_ORDER]
    kernel = functools.partial(_mix_prompt_kernel, tt=tt, tiles_per_seq=tiles_per_seq, alpha=alpha)
    return pl.pallas_call(
        kernel,
        out_shape=[jax.ShapeDtypeStruct((m, d), F32), jax.ShapeDtypeStruct((N_EXPERTS, m), F32)],
        grid=(m // tt,),
        in_specs=[row(y_att.shape[1]), row(pw), halo(POOL_HALO, pw), row(cw), halo(CONV_HALO, cw), row(d)]
                 + [_full_spec(w) for w in weights],
        out_specs=[row(d), pl.BlockSpec((N_EXPERTS, tt), lambda i: (0, i))],
        scratch_shapes=[pltpu.VMEM((POOL_HALO + tt, pw), F32), pltpu.VMEM((SUBLANES, CONV_HALO + tt, cw), F32),
                        pltpu.VMEM((tt, cw), F32)],
        compiler_params=_params("parallel"),
        name="mix_prompt",
    )(y_att, u_pool, u_pool, glu, glu, x, *weights)


def _mix_decode(y_att, u_pool, pool_state, glu, conv_state, x, mw, *, past_len, alpha):
    m, d = x.shape
    weights = [mw[k] for k in _MIX_W_ORDER]
    ins = [y_att, u_pool, pool_state, glu, conv_state, x] + weights
    kernel = functools.partial(_mix_decode_kernel, past_len=past_len, alpha=alpha)
    return pl.pallas_call(
        kernel,
        out_shape=[jax.ShapeDtypeStruct((m, d), F32), jax.ShapeDtypeStruct((N_EXPERTS, m), F32)],
        grid=(1,),
        in_specs=[_full_spec(a) for a in ins],
        out_specs=[pl.BlockSpec((m, d), lambda i: (0, 0)), pl.BlockSpec((N_EXPERTS, m), lambda i: (0, 0))],
        compiler_params=_params("arbitrary"),
        name="mix_decode",
    )(*ins)


def _moe_kernel(x_ref, gates_ref, wg_ref, wu_ref, wd_ref, ln_ref, o_ref, xb_ref, acc_ref, *, alpha):
    e = pl.program_id(1)

    @pl.when(e == 0)
    def _():
        xb_ref[...] = x_ref[...].astype(BF16)
        acc_ref[...] = jnp.zeros(acc_ref.shape, F32)

    xb = xb_ref[...]
    hg = jnp.dot(xb, wg_ref[0].astype(BF16), preferred_element_type=F32)
    hu = jnp.dot(xb, wu_ref[0].astype(BF16), preferred_element_type=F32)
    gates = gates_ref[...]
    lane = lax.broadcasted_iota(jnp.int32, gates.shape, 1)
    gate_e = jnp.sum(jnp.where(lane == e, gates, 0.0), axis=1, keepdims=True)
    hidden = (_silu(hg) * hu * gate_e).astype(BF16)
    acc_ref[...] += jnp.dot(hidden, wd_ref[0].astype(BF16), preferred_element_type=F32)

    @pl.when(e == pl.num_programs(1) - 1)
    def _():
        o_ref[...] = _layer_norm(alpha * x_ref[...] + acc_ref[...], ln_ref[0:1, :], ln_ref[1:2, :])


def _moe(x, gates, exp_gate, exp_up, exp_down, ln2, *, layer, alpha):
    m, d = x.shape
    tm = min(TM_MOE, m)
    assert m % tm == 0
    de = exp_gate.shape[-1]
    return pl.pallas_call(
        functools.partial(_moe_kernel, alpha=alpha),
        out_shape=jax.ShapeDtypeStruct((m, d), F32),
        grid=(m // tm, N_EXPERTS),
        in_specs=[
            pl.BlockSpec((tm, d), lambda i, e: (i, 0)),
            pl.BlockSpec((tm, N_EXPERTS), lambda i, e: (i, 0)),
            pl.BlockSpec((None, 1, d, de), lambda i, e: (layer, e, 0, 0)),
            pl.BlockSpec((None, 1, d, de), lambda i, e: (layer, e, 0, 0)),
            pl.BlockSpec((None, 1, de, d), lambda i, e: (layer, e, 0, 0)),
            pl.BlockSpec((2, d), lambda i, e: (0, 0)),
        ],
        out_specs=pl.BlockSpec((tm, d), lambda i, e: (i, 0)),
        scratch_shapes=[pltpu.VMEM((tm, d), BF16), pltpu.VMEM((tm, d), F32)],
        compiler_params=_params("parallel", "arbitrary"),
        name="moe",
    )(x, gates, exp_gate, exp_up, exp_down, ln2)


def _moe_plan(gates_t, tm, chunk):
    m = gates_t.shape[1]
    tiles = m // tm
    gsum = gates_t.reshape(N_EXPERT_GROUPS, EXPERTS_PER_GROUP, m).sum(axis=1)
    gid = jnp.argmax(gsum, axis=0).astype(jnp.int32).reshape(tiles, tm)
    onehot = (gid[:, None, :] == jnp.arange(N_EXPERT_GROUPS, dtype=jnp.int32)[None, :, None]).astype(jnp.int32)
    csum = jnp.cumsum(onehot, axis=2)
    nch = (csum[:, :, -1] + chunk - 1) // chunk
    start = (jnp.cumsum(nch, axis=1) - nch) * chunk
    dst = jnp.sum(onehot * (start[:, :, None] + csum - 1), axis=1)
    return dst.reshape(-1), start.reshape(-1), nch.reshape(-1)


def _moe_grouped_kernel(dst_ref, start_ref, nch_ref, x_ref, gates_ref, wg_ref, wu_ref, wd_ref, ln_ref,
                        o_ref, xs_ref, gs_ref, acc_ref, *, alpha, tm, chunk):
    i = pl.program_id(0)
    e = pl.program_id(1)
    half = x_ref.shape[1] // 2

    @pl.when(e == 0)
    def _():
        def pack(c, carry):
            r0 = pl.multiple_of(c * chunk, chunk)
            lo = pltpu.bitcast(x_ref[pl.ds(r0, chunk), :half].astype(BF16).astype(F32), jnp.uint32)
            hi = pltpu.bitcast(x_ref[pl.ds(r0, chunk), half:].astype(BF16).astype(F32), jnp.uint32)
            o_ref[pl.ds(r0, chunk), :half] = pltpu.bitcast((hi & jnp.uint32(0xFFFF0000)) | (lo >> 16), F32)
            return carry

        lax.fori_loop(0, tm // chunk, pack, 0)
        xs_ref[...] = jnp.zeros(xs_ref.shape, F32)
        gs_ref[...] = jnp.zeros(gs_ref.shape, F32)
        acc_ref[...] = jnp.zeros(acc_ref.shape, F32)

        def put(r, carry):
            d = dst_ref[i * tm + r]
            xs_ref[pl.ds(d, 1), :] = o_ref[pl.ds(r, 1), :half]
            gs_ref[pl.ds(d, 1), :] = gates_ref[pl.ds(r, 1), :]
            return carry

        lax.fori_loop(0, tm, put, 0, unroll=8)

    g = e // EXPERTS_PER_GROUP
    start = start_ref[i * N_EXPERT_GROUPS + g]
    wg = wg_ref[0].astype(BF16)
    wu = wu_ref[0].astype(BF16)
    wd = wd_ref[0].astype(BF16)
    lane = lax.broadcasted_iota(jnp.int32, (chunk, gs_ref.shape[1]), 1)

    def rows_of(c, carry):
        r0 = pl.multiple_of(start + c * chunk, chunk)
        words = pltpu.bitcast(xs_ref[pl.ds(r0, chunk), :], jnp.uint32)
        x_lo = pltpu.bitcast(words << 16, F32).astype(BF16)
        x_hi = pltpu.bitcast(words & jnp.uint32(0xFFFF0000), F32).astype(BF16)
        hg = (jnp.dot(x_lo, wg[:half], preferred_element_type=F32)
              + jnp.dot(x_hi, wg[half:], preferred_element_type=F32))
        hu = (jnp.dot(x_lo, wu[:half], preferred_element_type=F32)
              + jnp.dot(x_hi, wu[half:], preferred_element_type=F32))
        gate_e = jnp.sum(jnp.where(lane == e, gs_ref[pl.ds(r0, chunk), :], 0.0), axis=1, keepdims=True)
        hidden = (_silu(hg) * hu * gate_e).astype(BF16)
        acc_ref[pl.ds(r0, chunk), :] += jnp.dot(hidden, wd, preferred_element_type=F32)
        return carry

    lax.fori_loop(0, nch_ref[i * N_EXPERT_GROUPS + g], rows_of, 0)

    @pl.when(e == pl.num_programs(1) - 1)
    def _():
        def take(r, carry):
            d = dst_ref[i * tm + r]
            o_ref[pl.ds(r, 1), :] = acc_ref[pl.ds(d, 1), :]
            return carry

        lax.fori_loop(0, tm, take, 0, unroll=8)

        def norm(c, carry):
            rows = pl.ds(pl.multiple_of(c * chunk, chunk), chunk)
            o_ref[rows, :] = _layer_norm(alpha * x_ref[rows, :] + o_ref[rows, :], ln_ref[0:1, :], ln_ref[1:2, :])
            return carry

        lax.fori_loop(0, tm // chunk, norm, 0)


def _moe_grouped(x, gates_t, exp_gate, exp_up, exp_down, ln2, *, layer, alpha):
    m, d = x.shape
    tm, chunk = TM_MOE_GROUPED, MOE_CHUNK
    assert m % tm == 0
    de = exp_gate.shape[-1]
    sort_rows = tm + N_EXPERT_GROUPS * chunk
    dst, start, nch = _moe_plan(gates_t, tm, chunk)
    once = pl.Buffered(1)
    return pl.pallas_call(
        functools.partial(_moe_grouped_kernel, alpha=alpha, tm=tm, chunk=chunk),
        out_shape=jax.ShapeDtypeStruct((m, d), F32),
        grid_spec=pltpu.PrefetchScalarGridSpec(
            num_scalar_prefetch=3,
            grid=(m // tm, N_EXPERTS),
            in_specs=[
                pl.BlockSpec((tm, d), lambda i, e, *_: (i, 0), pipeline_mode=once),
                pl.BlockSpec((tm, N_EXPERTS), lambda i, e, *_: (i, 0), pipeline_mode=once),
                pl.BlockSpec((None, 1, d, de), lambda i, e, *_: (layer, e, 0, 0)),
                pl.BlockSpec((None, 1, d, de), lambda i, e, *_: (layer, e, 0, 0)),
                pl.BlockSpec((None, 1, de, d), lambda i, e, *_: (layer, e, 0, 0)),
                pl.BlockSpec((2, d), lambda i, e, *_: (0, 0)),
            ],
            out_specs=pl.BlockSpec((tm, d), lambda i, e, *_: (i, 0), pipeline_mode=once),
            scratch_shapes=[pltpu.VMEM((sort_rows, d // 2), F32), pltpu.VMEM((sort_rows, N_EXPERTS), F32),
                            pltpu.VMEM((sort_rows, d), F32)],
        ),
        compiler_params=pltpu.CompilerParams(dimension_semantics=("parallel", "arbitrary"),
                                             vmem_limit_bytes=VMEM_LIMIT_MOE),
        name="moe_grouped",
    )(dst, start, nch, x, gates_t.T, exp_gate, exp_up, exp_down, ln2)


def kernel(x_prompt, x_sample, cache_k, cache_v, state_pool, state_conv, page_table, w_in, pool_w, pool_scale, conv_w, conv_b, conv_ln_g, conv_ln_b, conv_pw, w_out, ln1_g, ln1_b, router_w, router_bias, exp_gate, exp_up, exp_down, ln2_g, ln2_b):
    b, t, d = x_prompt.shape
    bd, td, _ = x_sample.shape
    assert td == 1
    depth = w_in.shape[0]
    alpha = (2 * depth) ** 0.25
    past_len = page_table.shape[1] * cache_k.shape[2]

    hp = x_prompt.reshape(b * t, d)
    hs = x_sample.reshape(bd, d)
    ks_l, vs_l, pp_l, ps_l, cp_l, cs_l = [], [], [], [], [], []
    kv_all = None
    for l in range(depth):
        lw = dict(pool_w=pool_w[l], pool_scale=pool_scale[l], conv_w=conv_w[l], conv_b=conv_b[l],
                  conv_ln_g=conv_ln_g[l], conv_ln_b=conv_ln_b[l], conv_pw=conv_pw[l], w_out=w_out[l],
                  ln1_g=ln1_g[l], ln1_b=ln1_b[l], router_w=router_w, router_bias=router_bias)
        mw = _mix_weights(lw)
        w_in_bf = w_in[l].astype(BF16)
        wqkvt_bf = w_in_bf[:, :3 * ATT_WIDTH].T
        ln2 = jnp.stack([ln2_g[l], ln2_b[l]])

        qt, kb, kmean, vtb, u_pool, glu, *kv_all = _in_proj_prompt(hp, w_in_bf, wqkvt_bf, kv_all, b=b, t=t,
                                                                    layer=l, depth=depth)
        y_att = _moba_prompt(qt, kb, vtb, kmean, b, t)
        x1, gates_t = _mix_prompt(y_att, u_pool, glu, hp, mw, t=t, alpha=alpha)
        hp = _moe_grouped(x1, gates_t, exp_gate, exp_up, exp_down, ln2, layer=l, alpha=alpha)
        pp_l.append(u_pool.reshape(b, t, -1)[:, t - POOL_HIST:])
        cp_l.append(glu.reshape(b, t, -1)[:, t - CONV_HIST:])

        qs, ks, vs, us, gs = _in_proj_decode(hs, w_in_bf)
        ys = _moba_decode(qs, ks, vs, cache_k, cache_v, page_table, l)
        x1s, gates_ts = _mix_decode(ys, us, state_pool[:, l], gs, state_conv[:, l], hs, mw,
                                    past_len=past_len, alpha=alpha)
        hs = _moe(x1s, gates_ts.T, exp_gate, exp_up, exp_down, ln2, layer=l, alpha=alpha)
        ks_l.append(ks.reshape(bd, 1, ATT_HEADS, HEAD_DIM))
        vs_l.append(vs.reshape(bd, 1, ATT_HEADS, HEAD_DIM))
        ps_l.append(jnp.concatenate([state_pool[:, l, 1:], us[:, None, :]], axis=1))
        cs_l.append(jnp.concatenate([state_conv[:, l, 1:], gs[:, None, :]], axis=1))

    stack = lambda xs: jnp.stack(xs, axis=1)

    def heads_last(x):
        return jnp.transpose(x.reshape(b, depth, ATT_HEADS, HEAD_DIM, t), (0, 1, 4, 2, 3))

    return (hp.reshape(b, t, d), hs.reshape(bd, 1, d), heads_last(kv_all[0]), heads_last(kv_all[1]),
            stack(ks_l), stack(vs_l),
            stack(pp_l), stack(ps_l), stack(cp_l), stack(cs_l))
```
